```python
import jax
import jax.numpy as jnp
from jax import lax
import numpy as np

D_MODEL = 1024
BATCH = 4
SEQ = 4096
DEPTH = 4

PLE_DIM = 256
HEAD_DIM = 128
A_HEADS = D_MODEL // HEAD_DIM
A_WIDTH = A_HEADS * HEAD_DIM
B_HEADS = D_MODEL // HEAD_DIM
B_WIDTH = B_HEADS * HEAD_DIM
MIX_HEADS = A_HEADS + B_HEADS
EVEN_WIDTH = A_WIDTH + B_WIDTH
EVEN_IN = 3 * A_WIDTH + 3 * B_WIDTH + 2 * B_HEADS + EVEN_WIDTH
LRU_WIDTH = 2 * D_MODEL
LRU_BLOCKS = 16
LRU_BLOCK = LRU_WIDTH // LRU_BLOCKS
LRU_C = 8.0
CONV_WIDTH = 4
CHUNK = 64
N_EVEN = (DEPTH + 1) // 2
N_ODD = DEPTH // 2
EPS = 1e-6

kernel_name = "hybrid_hgrn2_gdn_rglru_trunk"


def rmsnorm(x, g):
    xf = x.astype(jnp.float32)
    y = xf * lax.rsqrt(jnp.mean(xf * xf, axis=-1, keepdims=True) + EPS)
    return (y * g.astype(jnp.float32)).astype(x.dtype)


def head_rmsnorm(o, g):
    h, d = o.shape[-2], o.shape[-1]
    y = o * lax.rsqrt(jnp.mean(o * o, axis=-1, keepdims=True) + EPS)
    return y * g.astype(jnp.float32).reshape(h, d)


def l2norm(t):
    return t * lax.rsqrt(jnp.sum(t * t, axis=-1, keepdims=True) + EPS)


def causal_dwconv(x, w):
    return lax.conv_general_dilated(
        x, w.astype(jnp.float32)[:, None, :], window_strides=(1,),
        padding=((CONV_WIDTH - 1, 0),), dimension_numbers=("NWC", "WIO", "NWC"),
        feature_group_count=x.shape[-1])


def to_chunks(t):
    b, s, h = t.shape[:3]
    t = t.reshape((b, s // CHUNK, CHUNK, h) + t.shape[3:])
    return jnp.moveaxis(t, (1, 2, 3), (0, 3, 2))


def from_chunks(o):
    o = jnp.moveaxis(o, (0, 2, 3), (1, 3, 2))
    b, n, c, h, d = o.shape
    return o.reshape(b, n * c, h, d)


def hgrn2_chunked(q, k, v, log_f):
    b, _, h, dk = q.shape
    dv = v.shape[-1]
    tri = jnp.tril(jnp.ones((CHUNK, CHUNK), dtype=bool))

    def step(state, xs):
        q_c, k_c, v_c, g_c = xs
        cum = jnp.cumsum(g_c, axis=-2)
        rel = jnp.where(tri[:, :, None], cum[..., :, None, :] - cum[..., None, :, :], -jnp.inf)
        attn = jnp.einsum("bhtd,bhsd,bhtsd->bhts", q_c, k_c, jnp.exp(rel))
        o = (jnp.einsum("bhts,bhse->bhte", attn, v_c)
             + jnp.einsum("bhtd,bhde->bhte", q_c * jnp.exp(cum), state))
        last = cum[..., -1:, :]
        state = (jnp.exp(last[..., 0, :])[..., None] * state
                 + jnp.einsum("bhsd,bhse->bhde", k_c * jnp.exp(last - cum), v_c))
        return state, o

    s0 = jnp.zeros((b, h, dk, dv), jnp.float32)
    _, o = lax.scan(step, s0, (to_chunks(q), to_chunks(k), to_chunks(v), to_chunks(log_f)))
    return from_chunks(o)


def gated_delta_chunked(q, k, v, g, beta):
    b, _, h, dk = q.shape
    dv = v.shape[-1]
    qc, kc, vc = to_chunks(q), to_chunks(k), to_chunks(v)
    gcum = jnp.cumsum(to_chunks(g), axis=-1)
    bc = to_chunks(beta)
    tri = jnp.tril(jnp.ones((CHUNK, CHUNK), dtype=bool))
    strict = jnp.tril(jnp.ones((CHUNK, CHUNK), dtype=bool), -1)
    decay = jnp.exp(jnp.where(tri, gcum[..., :, None] - gcum[..., None, :], -jnp.inf))
    kb = kc * bc[..., None]
    vb = vc * bc[..., None]
    a_strict = jnp.where(strict, jnp.einsum("nbhtd,nbhsd->nbhts", kb, kc) * decay, 0.0)
    m = a_strict + jnp.eye(CHUNK, dtype=jnp.float32)
    u = lax.linalg.triangular_solve(m, vb, left_side=True, lower=True, unit_diagonal=True)
    w = lax.linalg.triangular_solve(m, kb * jnp.exp(gcum)[..., None], left_side=True,
                                    lower=True, unit_diagonal=True)
    qk = jnp.einsum("nbhtd,nbhsd->nbhts", qc, kc) * decay

    def step(state, xs):
        q_c, k_c, u_c, w_c, qk_c, g_c = xs
        v_new = u_c - jnp.einsum("bhtd,bhde->bhte", w_c, state)
        o = (jnp.einsum("bhtd,bhde->bhte", q_c * jnp.exp(g_c)[..., None], state)
             + jnp.einsum("bhts,bhse->bhte", qk_c, v_new))
        last = g_c[..., -1:]
        state = (state * jnp.exp(last)[..., None]
                 + jnp.einsum("bhsd,bhse->bhde", k_c * jnp.exp(last - g_c)[..., None], v_new))
        return state, o

    s0 = jnp.zeros((b, h, dk, dv), jnp.float32)
    _, o = lax.scan(step, s0, (qc, kc, u, w, qk, gcum))
    return from_chunks(o)


def even_layer(x, norm_g, w_in, lb, conv_w, a_log, dt_bias, head_g, w_out):
    bsz, s, _ = x.shape
    z = (rmsnorm(x, norm_g) @ w_in).astype(jnp.float32)
    o1 = A_WIDTH
    o2 = 2 * A_WIDTH
    o3 = 3 * A_WIDTH
    o4 = o3 + 3 * B_WIDTH
    o5 = o4 + B_HEADS
    o6 = o5 + B_HEADS
    a_q, a_f, a_i = z[..., :o1], z[..., o1:o2], z[..., o2:o3]
    b_qkv, b_a, b_b, gate = z[..., o3:o4], z[..., o4:o5], z[..., o5:o6], z[..., o6:]

    lb = lb.astype(jnp.float32)
    log_f = jnp.logaddexp(jnp.log(lb), jnp.log1p(-lb) + jax.nn.log_sigmoid(a_f))
    a_k = (1.0 - lb) * jax.nn.sigmoid(-a_f)
    hd = (bsz, s, A_HEADS, HEAD_DIM)
    o_a = hgrn2_chunked(a_q.reshape(hd), a_k.reshape(hd), a_i.reshape(hd), log_f.reshape(hd))

    qkv = jax.nn.silu(causal_dwconv(b_qkv, conv_w))
    hb = (bsz, s, B_HEADS, HEAD_DIM)
    q = l2norm(qkv[..., :B_WIDTH].reshape(hb)) * (HEAD_DIM ** -0.5)
    k = l2norm(qkv[..., B_WIDTH:2 * B_WIDTH].reshape(hb))
    v = qkv[..., 2 * B_WIDTH:].reshape(hb)
    beta = jax.nn.sigmoid(b_b)
    g = -jnp.exp(a_log.astype(jnp.float32)) * jax.nn.softplus(b_a + dt_bias.astype(jnp.float32))
    o_b = gated_delta_chunked(q, k, v, g, beta)

    o = head_rmsnorm(jnp.concatenate([o_a, o_b], axis=2), head_g)
    y = o.reshape(bsz, s, EVEN_WIDTH) * jax.nn.silu(gate)
    return y.astype(x.dtype) @ w_out


def _lin_combine(c1, c2):
    a1, b1 = c1
    a2, b2 = c2
    return a1 * a2, a2 * b1 + b2


def odd_layer(x, norm_g, w_in, conv_w, conv_b, w_a, b_a, w_x, b_x, lam, w_out):
    bsz, s, _ = x.shape
    z = (rmsnorm(x, norm_g) @ w_in).astype(jnp.float32)
    xb, gate = z[..., :LRU_WIDTH], z[..., LRU_WIDTH:]
    xc = causal_dwconv(xb, conv_w) + conv_b.astype(jnp.float32)
    xh = xc.reshape(bsz, s, LRU_BLOCKS, LRU_BLOCK)
    r = jax.nn.sigmoid(jnp.einsum("bsgi,gij->bsgj", xh, w_a.astype(jnp.float32)).reshape(bsz, s, LRU_WIDTH)
                       + b_a.astype(jnp.float32))
    i = jax.nn.sigmoid(jnp.einsum("bsgi,gij->bsgj", xh, w_x.astype(jnp.float32)).reshape(bsz, s, LRU_WIDTH)
                       + b_x.astype(jnp.float32))
    log_a = -LRU_C * r * jax.nn.softplus(-lam.astype(jnp.float32))
    a = jnp.exp(log_a)
    mult = jnp.sqrt(-jnp.expm1(2.0 * log_a))
    mult = jnp.where(jnp.arange(s)[None, :, None] == 0, 1.0, mult)
    _, h = lax.associative_scan(_lin_combine, (a, xc * i * mult), axis=1)
    y = h * jax.nn.silu(gate)
    return y.astype(x.dtype) @ w_out


def setup_inputs(seed: int = 0) -> dict:
    key = jax.random.key(seed)
    ks = jax.random.split(key, 24)
    f32 = jnp.float32
    nrm = lambda k, shp: jax.random.normal(k, shp, f32)
    dt = jnp.exp(jax.random.uniform(ks[7], (N_EVEN, B_HEADS), f32,
                                    minval=float(np.log(1e-3)), maxval=float(np.log(1e-1))))
    a_base = jax.random.uniform(ks[18], (N_ODD, LRU_WIDTH), f32, minval=0.9, maxval=0.999)
    sig = a_base ** (1.0 / LRU_C)
    return {
        "x": nrm(ks[0], (BATCH, SEQ, D_MODEL)),
        "p": nrm(ks[1], (DEPTH, BATCH, SEQ, PLE_DIM)),
        "even_norm": 1.0 + 0.02 * nrm(ks[2], (N_EVEN, D_MODEL)),
        "even_w_in": nrm(ks[3], (N_EVEN, D_MODEL, EVEN_IN)) * D_MODEL ** -0.5,
        "hgrn_lb_logits": 0.5 * nrm(ks[4], (N_EVEN, A_WIDTH)),
        "gdn_conv_w": nrm(ks[5], (N_EVEN, CONV_WIDTH, 3 * B_WIDTH)) * CONV_WIDTH ** -0.5,
        "gdn_a_log": jnp.log(jax.random.uniform(ks[6], (N_EVEN, B_HEADS), f32, minval=1.0, maxval=16.0)),
        "gdn_dt_bias": dt + jnp.log(-jnp.expm1(-dt)),
        "even_head_norm": 1.0 + 0.02 * nrm(ks[8], (N_EVEN, EVEN_WIDTH)),
        "even_w_out": nrm(ks[9], (N_EVEN, EVEN_WIDTH, D_MODEL)) * EVEN_WIDTH ** -0.5,
        "odd_norm": 1.0 + 0.02 * nrm(ks[10], (N_ODD, D_MODEL)),
        "odd_w_in": nrm(ks[11], (N_ODD, D_MODEL, 2 * LRU_WIDTH)) * D_MODEL ** -0.5,
        "lru_conv_w": nrm(ks[12], (N_ODD, CONV_WIDTH, LRU_WIDTH)) * CONV_WIDTH ** -0.5,
        "lru_conv_b": 0.01 * nrm(ks[13], (N_ODD, LRU_WIDTH)),
        "lru_w_a": nrm(ks[14], (N_ODD, LRU_BLOCKS, LRU_BLOCK, LRU_BLOCK)) * LRU_BLOCK ** -0.5,
        "lru_b_a": 0.01 * nrm(ks[15], (N_ODD, LRU_WIDTH)),
        "lru_w_x": nrm(ks[16], (N_ODD, LRU_BLOCKS, LRU_BLOCK, LRU_BLOCK)) * LRU_BLOCK ** -0.5,
        "lru_b_x": 0.01 * nrm(ks[17], (N_ODD, LRU_WIDTH)),
        "lru_lambda": jnp.log(sig) - jnp.log1p(-sig),
        "odd_w_out": nrm(ks[19], (N_ODD, LRU_WIDTH, D_MODEL)) * LRU_WIDTH ** -0.5,
        "ple_norm": 1.0 + 0.02 * nrm(ks[20], (DEPTH, D_MODEL)),
        "ple_gate_w": nrm(ks[21], (DEPTH, D_MODEL, D_MODEL)) * D_MODEL ** -0.5,
        "ple_proj_w": nrm(ks[22], (DEPTH, PLE_DIM, D_MODEL)) * (0.5 * PLE_DIM ** -0.5),
        "final_norm": 1.0 + 0.02 * nrm(ks[23], (D_MODEL,)),
    }


def reference(x, p, even_norm, even_w_in, hgrn_lb_logits, gdn_conv_w, gdn_a_log, gdn_dt_bias,
              even_head_norm, even_w_out, odd_norm, odd_w_in, lru_conv_w, lru_conv_b, lru_w_a,
              lru_b_a, lru_w_x, lru_b_x, lru_lambda, odd_w_out, ple_norm, ple_gate_w, ple_proj_w,
              final_norm):
    lb_cum = jnp.cumsum(jax.nn.softmax(hgrn_lb_logits.astype(jnp.float32), axis=0), axis=0)
    lower_bounds = lb_cum - lb_cum[0]
    for i in range(DEPTH):
        j = i // 2
        if i % 2 == 0:
            x = x + even_layer(x, even_norm[j], even_w_in[j], lower_bounds[j], gdn_conv_w[j],
                               gdn_a_log[j], gdn_dt_bias[j], even_head_norm[j], even_w_out[j])
        else:
            x = x + odd_layer(x, odd_norm[j], odd_w_in[j], lru_conv_w[j], lru_conv_b[j], lru_w_a[j],
                              lru_b_a[j], lru_w_x[j], lru_b_x[j], lru_lambda[j], odd_w_out[j])
        ple_gate = jax.nn.sigmoid(rmsnorm(x, ple_norm[i]) @ ple_gate_w[i])
        x = x + ple_gate * (p[i] @ ple_proj_w[i])
    return rmsnorm(x, final_norm)
```

```python
import functools

import jax
import jax.numpy as jnp
from jax import lax
from jax.experimental import pallas as pl
from jax.experimental.pallas import tpu as pltpu

EPS = 1e-6
HEAD_DIM = 128
LANES = 128
SUBLANES = 8
CHUNK = 64
SUB = 16
CONV_WIDTH = 4
LRU_C = 8.0
VMEM_LIMIT = 48 * 1024 * 1024

_f32 = jnp.float32
_bf16 = jnp.bfloat16


def _bf(x):
    return x.astype(_bf16)


def _dot(a, b):
    return jnp.dot(a, b, preferred_element_type=_f32)


def _dot_nt(a, b):
    return lax.dot_general(a, b, (((1,), (1,)), ((), ())), preferred_element_type=_f32)


def _dot_tn(a, b):
    return lax.dot_general(a, b, (((0,), (0,)), ((), ())), preferred_element_type=_f32)


def _dot_hi(a, b):
    return jnp.dot(a, b, preferred_element_type=_f32, precision=lax.Precision.HIGHEST)


def _split3(x):
    hi = _bf(x)
    r = x - hi.astype(_f32)
    mid = _bf(r)
    lo = _bf(r - mid.astype(_f32))
    return hi, mid, lo


def _dot_exact_lhs(m_bf, x):
    hi, mid, lo = _split3(x)
    return _dot(m_bf, hi) + _dot(m_bf, mid) + _dot(m_bf, lo)


def _softplus(x):
    return jnp.maximum(x, 0.0) + jnp.log1p(jnp.exp(-jnp.abs(x)))


def _sigmoid(x):
    return jax.nn.sigmoid(x)


def _silu(x):
    return x * _sigmoid(x)


def _rmsnorm(x, g):
    ms = jnp.mean(x * x, axis=-1, keepdims=True)
    return x * lax.rsqrt(ms + EPS) * g


def _norm_inproj_kernel(*refs, has_small):
    if has_small:
        x_ref, g_ref, w_ref, ws_ref, z_ref, zs_ref, xn_ref = refs
    else:
        x_ref, g_ref, w_ref, z_ref, xn_ref = refs

    @pl.when(pl.program_id(1) == 0)
    def _():
        xn_ref[...] = _bf(_rmsnorm(x_ref[...], g_ref[...]))
        if has_small:
            zs_ref[...] = _dot(xn_ref[...], ws_ref[...])

    z_ref[...] = _dot(xn_ref[...], w_ref[...])


def _norm_inproj(x2d, g, w_bf, ws_bf=None, *, tm=1024, tn=1024):
    m, d = x2d.shape
    n = w_bf.shape[1]
    tm = min(tm, m)
    has_small = ws_bf is not None
    in_specs = [
        pl.BlockSpec((tm, d), lambda i, j: (i, 0)),
        pl.BlockSpec((1, d), lambda i, j: (0, 0)),
        pl.BlockSpec((d, tn), lambda i, j: (0, j)),
    ]
    args = [x2d, g.reshape(1, d), w_bf]
    out_shape = [jax.ShapeDtypeStruct((m, n), _f32)]
    out_specs = [pl.BlockSpec((tm, tn), lambda i, j: (i, j))]
    if has_small:
        ns = ws_bf.shape[1]
        in_specs.append(pl.BlockSpec((d, ns), lambda i, j: (0, 0)))
        args.append(ws_bf)
        out_shape.append(jax.ShapeDtypeStruct((m, ns), _f32))
        out_specs.append(pl.BlockSpec((tm, ns), lambda i, j: (i, 0)))
    return pl.pallas_call(
        functools.partial(_norm_inproj_kernel, has_small=has_small),
        grid=(m // tm, n // tn),
        in_specs=in_specs,
        out_specs=out_specs,
        out_shape=out_shape,
        scratch_shapes=[pltpu.VMEM((tm, d), _bf16)],
        compiler_params=pltpu.CompilerParams(
            dimension_semantics=("parallel", "arbitrary"), vmem_limit_bytes=VMEM_LIMIT),
        name="norm_inproj",
    )(*args)


def _outproj_ple_kernel(*refs, n_y, final):
    y_refs = refs[:n_y]
    w_refs = refs[n_y:2 * n_y]
    x_ref, p_ref, png_ref, wg_ref, wp_ref = refs[2 * n_y:2 * n_y + 5]
    rest = refs[2 * n_y + 5:]
    if final:
        fng_ref, out_ref = rest
    else:
        (out_ref,) = rest
    acc = x_ref[...]
    for y_ref, w_ref in zip(y_refs, w_refs):
        acc = acc + _dot(y_ref[...], w_ref[...])
    xn = _rmsnorm(acc, png_ref[...])
    gate = _sigmoid(_dot(_bf(xn), wg_ref[...]))
    pp = _dot(_bf(p_ref[...]), wp_ref[...])
    x2 = acc + gate * pp
    if final:
        x2 = _rmsnorm(x2, fng_ref[...])
    out_ref[...] = x2


def _outproj_ple(ys, ws_bf, x2d, p2d, png, wg_bf, wp_bf, fng=None, *, tm=512):
    m, d = x2d.shape
    tm = min(tm, m)
    n_y = len(ys)
    final = fng is not None
    in_specs, args = [], []
    for y in ys:
        in_specs.append(pl.BlockSpec((tm, y.shape[1]), lambda i: (i, 0)))
        args.append(y)
    for w in ws_bf:
        in_specs.append(pl.BlockSpec(w.shape, lambda i: (0, 0)))
        args.append(w)
    in_specs += [
        pl.BlockSpec((tm, d), lambda i: (i, 0)),
        pl.BlockSpec((tm, p2d.shape[1]), lambda i: (i, 0)),
        pl.BlockSpec((1, d), lambda i: (0, 0)),
        pl.BlockSpec(wg_bf.shape, lambda i: (0, 0)),
        pl.BlockSpec(wp_bf.shape, lambda i: (0, 0)),
    ]
    args += [x2d, p2d, png.reshape(1, d), wg_bf, wp_bf]
    if final:
        in_specs.append(pl.BlockSpec((1, d), lambda i: (0, 0)))
        args.append(fng.reshape(1, d))
    return pl.pallas_call(
        functools.partial(_outproj_ple_kernel, n_y=n_y, final=final),
        grid=(m // tm,),
        in_specs=in_specs,
        out_specs=pl.BlockSpec((tm, d), lambda i: (i, 0)),
        out_shape=jax.ShapeDtypeStruct((m, d), _f32),
        compiler_params=pltpu.CompilerParams(
            dimension_semantics=("parallel",), vmem_limit_bytes=VMEM_LIMIT),
        name="outproj_ple",
    )(*args)


def _causal_conv(x, prev, w):
    t = x.shape[0]
    xp = jnp.concatenate([prev, x], axis=0)
    base = SUBLANES - (CONV_WIDTH - 1)
    y = w[CONV_WIDTH - 1:CONV_WIDTH] * x
    for j in range(CONV_WIDTH - 1):
        y = y + w[j:j + 1] * xp[base + j:base + j + t]
    return y


def _hgrn_chunk(q, k, v, logf, st, tri_bf):
    c = q.shape[0]
    cum = _dot_exact_lhs(tri_bf, logf)
    lane = lax.broadcasted_iota(jnp.int32, (SUB, c), 1)
    row = lax.broadcasted_iota(jnp.int32, (SUB, 1), 0)
    blocks = []
    for i in range(c // SUB):
        lo = i * SUB
        cs = cum[lo:lo + SUB]
        qs = q[lo:lo + SUB]
        a_i = jnp.zeros((SUB, c), _f32)
        if i > 0:
            anchor = cum[lo:lo + 1]
            qa = qs * jnp.exp(cs - anchor)
            ka = k * jnp.exp(jnp.minimum(anchor - cum, 0.0))
            a_i = jnp.where(lane < lo, _dot_nt(_bf(qa), _bf(ka)), 0.0)
        for s in range(SUB):
            r = lo + s
            e = jnp.exp(jnp.minimum(cs - cum[r:r + 1], 0.0))
            col = jnp.sum(qs * (k[r:r + 1] * e), axis=-1, keepdims=True)
            col = jnp.where(row >= s, col, 0.0)
            a_i = jnp.where(lane == r, col, a_i)
        blocks.append(a_i)
    a = jnp.concatenate(blocks, axis=0)
    o = _dot(_bf(a), _bf(v)) + _dot_nt(_bf(q * jnp.exp(cum)), _bf(st))
    last = cum[c - 1:c]
    st_new = st * jnp.exp(last) + _dot_tn(_bf(v), _bf(k * jnp.exp(last - cum)))
    return o, st_new


def _unit_lower_inverse(a):
    c = a.shape[0]
    row = lax.broadcasted_iota(jnp.int32, (c, c), 0)
    col = lax.broadcasted_iota(jnp.int32, (c, c), 1)
    eye = (row == col).astype(_f32)
    d = jnp.where((row // SUB) == (col // SUB), a, 0.0)
    l = a - d
    d2 = _dot_hi(d, d)
    d4 = _dot_hi(d2, d2)
    d8 = _dot_hi(d4, d4)
    td = _dot_hi(_dot_hi(_dot_hi(eye - d, eye + d2), eye + d4), eye + d8)
    n = _dot_hi(td, l)
    n2 = _dot_hi(n, n)
    return _dot_hi(_dot_hi(eye - n, eye + n2), td)


def _gdn_chunk(q, k, v, gb, betab, s, tri_bf, ones_bf):
    c = q.shape[0]
    row = lax.broadcasted_iota(jnp.int32, (c, c), 0)
    col = lax.broadcasted_iota(jnp.int32, (c, c), 1)
    gc = _dot_exact_lhs(tri_bf, gb)
    gc_t = gc[:, :c]
    gc_s = _dot_exact_lhs(ones_bf, jnp.where(row <= col, gb[:, :c], 0.0))
    decay = jnp.where(row >= col, jnp.exp(jnp.minimum(gc_t - gc_s, 0.0)), 0.0)
    kb = k * betab
    vb = v * betab
    a = jnp.where(row > col, _dot_nt(_bf(kb), _bf(k)) * decay, 0.0)
    t_inv = _unit_lower_inverse(a)
    uw = _dot_hi(t_inv, jnp.concatenate([vb, kb * jnp.exp(gc)], axis=1))
    u = uw[:, :HEAD_DIM]
    w = uw[:, HEAD_DIM:]
    qk = _dot_nt(_bf(q), _bf(k)) * decay
    s_bf = _bf(s)
    v_new = u - _dot(_bf(w), s_bf)
    o = _dot(_bf(q * jnp.exp(gc)), s_bf) + _dot(_bf(qk), _bf(v_new))
    last = gc[c - 1:c]
    s_new = s * jnp.exp(last) + _dot_tn(_bf(k * jnp.exp(last - gc)), _bf(v_new))
    return o, s_new


def _even_rec_kernel(aq_ref, af_ref, ai_ref, bq_ref, bk_ref, bv_ref, ga_ref, gb_ref, zs_ref,
                     lbc_ref, cwq_ref, cwk_ref, cwv_ref, gp_ref, hga_ref, hgb_ref,
                     ya_ref, yb_ref,
                     st_ref, s_ref, carry_ref, logf_ref, kh_ref, qn_ref, kn_ref, vn_ref,
                     gcol_ref, bcol_ref):
    h = pl.program_id(1)
    tb = aq_ref.shape[0]
    n_chunks = tb // CHUNK

    @pl.when(pl.program_id(2) == 0)
    def _():
        st_ref[...] = jnp.zeros_like(st_ref)
        s_ref[...] = jnp.zeros_like(s_ref)
        carry_ref[...] = jnp.zeros_like(carry_ref)

    af = af_ref[...]
    log_lb = lbc_ref[0:1]
    log1m_lb = lbc_ref[1:2]
    onem_lb = lbc_ref[2:3]
    log_sig = jnp.minimum(af, 0.0) - jnp.log1p(jnp.exp(-jnp.abs(af)))
    bterm = log1m_lb + log_sig
    logf_ref[...] = (jnp.maximum(log_lb, bterm)
                     + jnp.log1p(jnp.exp(-jnp.abs(log_lb - bterm))))
    kh_ref[...] = onem_lb * _sigmoid(-af)

    xq = bq_ref[...]
    xk = bk_ref[...]
    xv = bv_ref[...]
    cq = _silu(_causal_conv(xq, carry_ref[0], cwq_ref[...]))
    ck = _silu(_causal_conv(xk, carry_ref[1], cwk_ref[...]))
    cv = _silu(_causal_conv(xv, carry_ref[2], cwv_ref[...]))
    carry_ref[0] = xq[tb - SUBLANES:]
    carry_ref[1] = xk[tb - SUBLANES:]
    carry_ref[2] = xv[tb - SUBLANES:]
    qn_ref[...] = cq * (lax.rsqrt(jnp.sum(cq * cq, axis=-1, keepdims=True) + EPS)
                        * (HEAD_DIM ** -0.5))
    kn_ref[...] = ck * lax.rsqrt(jnp.sum(ck * ck, axis=-1, keepdims=True) + EPS)
    vn_ref[...] = cv

    zs = zs_ref[...]
    lane = lax.broadcasted_iota(jnp.int32, zs.shape, 1)
    g_all = -jnp.exp(gp_ref[0:1]) * _softplus(zs + gp_ref[1:2])
    beta_all = _sigmoid(zs)
    n_heads = pl.num_programs(1)
    g_col = jnp.sum(jnp.where(lane == h, g_all, 0.0), axis=-1, keepdims=True)
    b_col = jnp.sum(jnp.where(lane == h + n_heads, beta_all, 0.0), axis=-1, keepdims=True)
    gcol_ref[...] = jnp.broadcast_to(g_col, zs.shape)
    bcol_ref[...] = jnp.broadcast_to(b_col, zs.shape)

    r_i = lax.broadcasted_iota(jnp.int32, (CHUNK, CHUNK), 0)
    c_i = lax.broadcasted_iota(jnp.int32, (CHUNK, CHUNK), 1)
    tri_bf = (r_i >= c_i).astype(_bf16)
    ones_bf = jnp.ones((CHUNK, CHUNK), _bf16)
    hga = hga_ref[0:1]
    hgb = hgb_ref[0:1]

    def head_out(o, hg, gate):
        y = o * lax.rsqrt(jnp.mean(o * o, axis=-1, keepdims=True) + EPS) * hg
        return _bf(y * _silu(gate))

    def body(ci, carry):
        sl = pl.ds(pl.multiple_of(ci * CHUNK, CHUNK), CHUNK)
        o_a, st_new = _hgrn_chunk(aq_ref[sl, :], kh_ref[sl, :], ai_ref[sl, :], logf_ref[sl, :],
                                  st_ref[...], tri_bf)
        st_ref[...] = st_new
        ya_ref[sl, :] = head_out(o_a, hga, ga_ref[sl, :])
        o_b, s_new = _gdn_chunk(qn_ref[sl, :], kn_ref[sl, :], vn_ref[sl, :], gcol_ref[sl, :],
                                bcol_ref[sl, :], s_ref[...], tri_bf, ones_bf)
        s_ref[...] = s_new
        yb_ref[sl, :] = head_out(o_b, hgb, gb_ref[sl, :])
        return carry

    lax.fori_loop(0, n_chunks, body, 0)


def _even_rec(z, zs, lbc, cw, gp, hg, *, batch, seq, tb=256):
    m = z.shape[0]
    n_heads = lbc.shape[1] // HEAD_DIM
    tb = min(tb, seq)
    nt = seq // tb

    def zspec(k):
        return pl.BlockSpec((tb, HEAD_DIM), lambda b, h, t, k=k: (b * nt + t, k * n_heads + h))

    def pspec(k):
        return pl.BlockSpec((SUBLANES, HEAD_DIM), lambda b, h, t, k=k: (0, k * n_heads + h))

    in_specs = [zspec(k) for k in range(8)]
    in_specs.append(pl.BlockSpec((tb, LANES), lambda b, h, t: (b * nt + t, 0)))
    in_specs.append(pspec(0))
    in_specs += [pspec(0), pspec(1), pspec(2)]
    in_specs.append(pl.BlockSpec((SUBLANES, LANES), lambda b, h, t: (0, 0)))
    in_specs += [pspec(0), pspec(1)]
    out_spec = pl.BlockSpec((tb, HEAD_DIM), lambda b, h, t: (b * nt + t, h))
    blk = pltpu.VMEM((tb, HEAD_DIM), _f32)
    return pl.pallas_call(
        _even_rec_kernel,
        grid=(batch, n_heads, nt),
        in_specs=in_specs,
        out_specs=[out_spec, out_spec],
        out_shape=[jax.ShapeDtypeStruct((m, n_heads * HEAD_DIM), _bf16)] * 2,
        scratch_shapes=[
            pltpu.VMEM((HEAD_DIM, HEAD_DIM), _f32),
            pltpu.VMEM((HEAD_DIM, HEAD_DIM), _f32),
            pltpu.VMEM((3, SUBLANES, HEAD_DIM), _f32),
            blk, blk, blk, blk, blk, blk, blk,
        ],
        compiler_params=pltpu.CompilerParams(
            dimension_semantics=("parallel", "parallel", "arbitrary"),
            vmem_limit_bytes=VMEM_LIMIT),
        name="even_rec",
    )(*([z] * 8), zs, lbc, cw, cw, cw, gp, hg, hg)


def _odd_rec_kernel(xb_ref, gate_ref, cp_ref, wa_ref, wx_ref, y_ref, carry_ref, h_ref):
    t_idx = pl.program_id(2)
    tb = xb_ref.shape[0]

    @pl.when(t_idx == 0)
    def _():
        carry_ref[...] = jnp.zeros_like(carry_ref)
        h_ref[...] = jnp.zeros_like(h_ref)

    x = xb_ref[...]
    xc = _causal_conv(x, carry_ref[...], cp_ref[...]) + cp_ref[4:5]
    carry_ref[...] = x[tb - SUBLANES:]
    xc_bf = _bf(xc)
    r = _sigmoid(_dot(xc_bf, wa_ref[0]) + cp_ref[5:6])
    i = _sigmoid(_dot(xc_bf, wx_ref[0]) + cp_ref[6:7])
    log_a = -LRU_C * r * _softplus(-cp_ref[7:8])
    a = jnp.exp(log_a)
    mult = jnp.sqrt(-jnp.tanh(log_a) * (a * a + 1.0))
    row = lax.broadcasted_iota(jnp.int32, x.shape, 0)
    mult = jnp.where(row + t_idx * tb == 0, 1.0, mult)
    b = xc * i * mult

    sh = 1
    while sh < tb:
        a_sh = jnp.where(row >= sh, pltpu.roll(a, sh, 0), 1.0)
        b_sh = jnp.where(row >= sh, pltpu.roll(b, sh, 0), 0.0)
        b = a * b_sh + b
        a = a * a_sh
        sh *= 2
    hcur = b + a * h_ref[0:1]
    h_ref[...] = jnp.broadcast_to(hcur[tb - 1:tb], h_ref.shape)
    y_ref[...] = _bf(hcur * _silu(gate_ref[...]))


def _odd_rec(z, cp, wa_bf, wx_bf, *, batch, seq, tb=256):
    m = z.shape[0]
    n_blocks = wa_bf.shape[0]
    tb = min(tb, seq)
    nt = seq // tb
    in_specs = [
        pl.BlockSpec((tb, LANES), lambda b, g, t: (b * nt + t, g)),
        pl.BlockSpec((tb, LANES), lambda b, g, t: (b * nt + t, n_blocks + g)),
        pl.BlockSpec((SUBLANES, LANES), lambda b, g, t: (0, g)),
        pl.BlockSpec((1, LANES, LANES), lambda b, g, t: (g, 0, 0)),
        pl.BlockSpec((1, LANES, LANES), lambda b, g, t: (g, 0, 0)),
    ]
    return pl.pallas_call(
        _odd_rec_kernel,
        grid=(batch, n_blocks, nt),
        in_specs=in_specs,
        out_specs=pl.BlockSpec((tb, LANES), lambda b, g, t: (b * nt + t, g)),
        out_shape=jax.ShapeDtypeStruct((m, n_blocks * LANES), _bf16),
        scratch_shapes=[pltpu.VMEM((SUBLANES, LANES), _f32), pltpu.VMEM((SUBLANES, LANES), _f32)],
        compiler_params=pltpu.CompilerParams(
            dimension_semantics=("parallel", "parallel", "arbitrary"),
            vmem_limit_bytes=VMEM_LIMIT),
        name="odd_rec",
    )(z, z, cp, wa_bf, wx_bf)


def _pad_rows(a, rows=SUBLANES):
    return jnp.pad(a, ((0, rows - a.shape[0]), (0, 0)))


def kernel(x, p, even_norm, even_w_in, hgrn_lb_logits, gdn_conv_w, gdn_a_log, gdn_dt_bias,
           even_head_norm, even_w_out, odd_norm, odd_w_in, lru_conv_w, lru_conv_b, lru_w_a,
           lru_b_a, lru_w_x, lru_b_x, lru_lambda, odd_w_out, ple_norm, ple_gate_w, ple_proj_w,
           final_norm):
    batch, seq, d = x.shape
    depth = p.shape[0]
    m = batch * seq
    a_width = hgrn_lb_logits.shape[1]
    n_heads = a_width // HEAD_DIM
    qkv_end = 6 * a_width
    small_end = qkv_end + 2 * n_heads

    lb_cum = jnp.cumsum(jax.nn.softmax(hgrn_lb_logits.astype(_f32), axis=0), axis=0)
    lower_bounds = lb_cum - lb_cum[0]

    x2d = x.reshape(m, d)
    for i in range(depth):
        j = i // 2
        if i % 2 == 0:
            w_in = even_w_in[j]
            w_main = _bf(jnp.concatenate([w_in[:, :qkv_end], w_in[:, small_end:]], axis=1))
            w_small = _bf(jnp.pad(w_in[:, qkv_end:small_end], ((0, 0), (0, LANES - 2 * n_heads))))
            z, zs = _norm_inproj(x2d, even_norm[j], w_main, w_small)
            lb = lower_bounds[j]
            lbc = _pad_rows(jnp.stack([jnp.log(lb), jnp.log1p(-lb), 1.0 - lb]))
            cw = _pad_rows(gdn_conv_w[j].astype(_f32))
            gp = _pad_rows(jnp.pad(jnp.stack([gdn_a_log[j], gdn_dt_bias[j]]).astype(_f32),
                                   ((0, 0), (0, LANES - n_heads))))
            hg = _pad_rows(even_head_norm[j].astype(_f32)[None, :])
            ya, yb = _even_rec(z, zs, lbc, cw, gp, hg, batch=batch, seq=seq)
            w_out = _bf(even_w_out[j])
            ys, ws = [ya, yb], [w_out[:a_width], w_out[a_width:]]
        else:
            z = _norm_inproj(x2d, odd_norm[j], _bf(odd_w_in[j]))[0]
            cp = jnp.concatenate([
                lru_conv_w[j], lru_conv_b[j][None], lru_b_a[j][None], lru_b_x[j][None],
                lru_lambda[j][None]], axis=0).astype(_f32)
            y = _odd_rec(z, cp, _bf(lru_w_a[j]), _bf(lru_w_x[j]), batch=batch, seq=seq)
            ys, ws = [y], [_bf(odd_w_out[j])]
        x2d = _outproj_ple(ys, ws, x2d, p[i].reshape(m, -1), ple_norm[i], _bf(ple_gate_w[i]),
                           _bf(ple_proj_w[i]), final_norm if i == depth - 1 else None)
    return x2d.reshape(batch, seq, d)
```

```python
import functools

import jax
import jax.numpy as jnp
from jax import lax
from jax.experimental import pallas as pl
from jax.experimental.pallas import tpu as pltpu

EPS = 1e-6
HEAD_DIM = 128
LANES = 128
SUBLANES = 8
CHUNK = 64
SUB = 16
CONV_WIDTH = 4
LRU_C = 8.0
VMEM_LIMIT = 48 * 1024 * 1024

_f32 = jnp.float32
_bf16 = jnp.bfloat16


def _bf(x):
    return x.astype(_bf16)


def _dot(a, b):
    return jnp.dot(a, b, preferred_element_type=_f32)


def _dot_nt(a, b):
    return lax.dot_general(a, b, (((1,), (1,)), ((), ())), preferred_element_type=_f32)


def _dot_tn(a, b):
    return lax.dot_general(a, b, (((0,), (0,)), ((), ())), preferred_element_type=_f32)


def _mm(a, b):
    return _dot(_bf(a), _bf(b))


def _softplus(x):
    return jnp.maximum(x, 0.0) + jnp.log1p(jnp.exp(-jnp.abs(x)))


def _sigmoid(x):
    return jax.nn.sigmoid(x)


def _silu(x):
    return x * _sigmoid(x)


def _rmsnorm(x, g):
    ms = jnp.mean(x * x, axis=-1, keepdims=True)
    return x * lax.rsqrt(ms + EPS) * g


def _norm_inproj_kernel(*refs, has_small):
    if has_small:
        x_ref, g_ref, w_ref, ws_ref, z_ref, zs_ref, xn_ref = refs
    else:
        x_ref, g_ref, w_ref, z_ref, xn_ref = refs

    @pl.when(pl.program_id(1) == 0)
    def _():
        xn_ref[...] = _bf(_rmsnorm(x_ref[...], g_ref[...]))
        if has_small:
            zs_ref[...] = _dot(xn_ref[...], ws_ref[...])

    z_ref[...] = _dot(xn_ref[...], w_ref[...])


def _norm_inproj(x2d, g, w_bf, ws_bf=None, *, tm=1024, tn=1024):
    m, d = x2d.shape
    n = w_bf.shape[1]
    tm = min(tm, m)
    has_small = ws_bf is not None
    in_specs = [
        pl.BlockSpec((tm, d), lambda i, j: (i, 0)),
        pl.BlockSpec((1, d), lambda i, j: (0, 0)),
        pl.BlockSpec((d, tn), lambda i, j: (0, j)),
    ]
    args = [x2d, g.reshape(1, d), w_bf]
    out_shape = [jax.ShapeDtypeStruct((m, n), _f32)]
    out_specs = [pl.BlockSpec((tm, tn), lambda i, j: (i, j))]
    if has_small:
        ns = ws_bf.shape[1]
        in_specs.append(pl.BlockSpec((d, ns), lambda i, j: (0, 0)))
        args.append(ws_bf)
        out_shape.append(jax.ShapeDtypeStruct((m, ns), _f32))
        out_specs.append(pl.BlockSpec((tm, ns), lambda i, j: (i, 0)))
    return pl.pallas_call(
        functools.partial(_norm_inproj_kernel, has_small=has_small),
        grid=(m // tm, n // tn),
        in_specs=in_specs,
        out_specs=out_specs,
        out_shape=out_shape,
        scratch_shapes=[pltpu.VMEM((tm, d), _bf16)],
        compiler_params=pltpu.CompilerParams(
            dimension_semantics=("parallel", "arbitrary"), vmem_limit_bytes=VMEM_LIMIT),
        name="norm_inproj",
    )(*args)


def _outproj_ple_kernel(*refs, n_y, final):
    y_refs = refs[:n_y]
    w_refs = refs[n_y:2 * n_y]
    x_ref, p_ref, png_ref, wg_ref, wp_ref = refs[2 * n_y:2 * n_y + 5]
    rest = refs[2 * n_y + 5:]
    if final:
        fng_ref, out_ref = rest
    else:
        (out_ref,) = rest
    acc = x_ref[...]
    for y_ref, w_ref in zip(y_refs, w_refs):
        acc = acc + _dot(y_ref[...], w_ref[...])
    xn = _rmsnorm(acc, png_ref[...])
    gate = _sigmoid(_dot(_bf(xn), wg_ref[...]))
    pp = _dot(_bf(p_ref[...]), wp_ref[...])
    x2 = acc + gate * pp
    if final:
        x2 = _rmsnorm(x2, fng_ref[...])
    out_ref[...] = x2


def _outproj_ple(ys, ws_bf, x2d, p2d, png, wg_bf, wp_bf, fng=None, *, tm=512):
    m, d = x2d.shape
    tm = min(tm, m)
    n_y = len(ys)
    final = fng is not None
    in_specs, args = [], []
    for y in ys:
        in_specs.append(pl.BlockSpec((tm, y.shape[1]), lambda i: (i, 0)))
        args.append(y)
    for w in ws_bf:
        in_specs.append(pl.BlockSpec(w.shape, lambda i: (0, 0)))
        args.append(w)
    in_specs += [
        pl.BlockSpec((tm, d), lambda i: (i, 0)),
        pl.BlockSpec((tm, p2d.shape[1]), lambda i: (i, 0)),
        pl.BlockSpec((1, d), lambda i: (0, 0)),
        pl.BlockSpec(wg_bf.shape, lambda i: (0, 0)),
        pl.BlockSpec(wp_bf.shape, lambda i: (0, 0)),
    ]
    args += [x2d, p2d, png.reshape(1, d), wg_bf, wp_bf]
    if final:
        in_specs.append(pl.BlockSpec((1, d), lambda i: (0, 0)))
        args.append(fng.reshape(1, d))
    return pl.pallas_call(
        functools.partial(_outproj_ple_kernel, n_y=n_y, final=final),
        grid=(m // tm,),
        in_specs=in_specs,
        out_specs=pl.BlockSpec((tm, d), lambda i: (i, 0)),
        out_shape=jax.ShapeDtypeStruct((m, d), _f32),
        compiler_params=pltpu.CompilerParams(
            dimension_semantics=("parallel",), vmem_limit_bytes=VMEM_LIMIT),
        name="outproj_ple",
    )(*args)


def _causal_conv(x, prev, w):
    t = x.shape[0]
    xp = jnp.concatenate([prev, x], axis=0)
    base = SUBLANES - (CONV_WIDTH - 1)
    y = w[CONV_WIDTH - 1:CONV_WIDTH] * x
    for j in range(CONV_WIDTH - 1):
        y = y + w[j:j + 1] * xp[base + j:base + j + t]
    return y


def _chunk_cumsum(x, row_in_chunk):
    sh = 1
    while sh < CHUNK:
        x = x + jnp.where(row_in_chunk >= sh, pltpu.roll(x, sh, 0), 0.0)
        sh *= 2
    return x


def _bcast_rows(x, starts, n):
    return jnp.concatenate(
        [jnp.broadcast_to(x[s:s + 1], (n, x.shape[1])) for s in starts], axis=0)


def _hgrn_block(q, k, v, logf, st, row_in_chunk, same_chunk, r_sub, c_sub):
    tb = q.shape[0]
    n_chunks = tb // CHUNK
    n_sub = CHUNK // SUB
    cum = _chunk_cumsum(logf, row_in_chunk)
    last = _bcast_rows(cum, [c * CHUNK + CHUNK - 1 for c in range(n_chunks)], CHUNK)

    anchor = _bcast_rows(cum, range(0, tb, SUB), SUB)
    qa = q * jnp.exp(cum - anchor)
    kas = []
    for i in range(1, n_sub):
        anc_i = _bcast_rows(cum, [c * CHUNK + i * SUB for c in range(n_chunks)], CHUNK)
        kas.append(k * jnp.exp(jnp.minimum(anc_i - cum, 0.0)))
    off = _dot_nt(_bf(qa), _bf(jnp.concatenate(kas, axis=0)))
    sel = off[:, (n_sub - 2) * tb:]
    for i in range(n_sub - 2, 0, -1):
        sel = jnp.where(r_sub == i, off[:, (i - 1) * tb:i * tb], sel)
    a_off = jnp.where(same_chunk, jnp.where(c_sub < r_sub, sel, 0.0), 0.0)
    o = _mm(a_off, v)

    row16 = lax.broadcasted_iota(jnp.int32, (SUB, 1), 0)
    diag = []
    for lo in range(0, tb, SUB):
        cs = cum[lo:lo + SUB]
        qs = q[lo:lo + SUB]
        acc = jnp.zeros((SUB, HEAD_DIM), _f32)
        for s in range(SUB):
            r = lo + s
            e = jnp.exp(jnp.minimum(cs - cum[r:r + 1], 0.0))
            col = jnp.sum(qs * (k[r:r + 1] * e), axis=-1, keepdims=True)
            acc = acc + jnp.where(row16 >= s, col, 0.0) * v[r:r + 1]
        diag.append(acc)
    o = o + jnp.concatenate(diag, axis=0)

    qd = _bf(q * jnp.exp(cum))
    kd = _bf(k * jnp.exp(last - cum))
    v_bf = _bf(v)
    inter = []
    for c in range(n_chunks):
        sl = slice(c * CHUNK, (c + 1) * CHUNK)
        inter.append(_dot_nt(qd[sl], _bf(st)))
        st = st * jnp.exp(last[c * CHUNK:c * CHUNK + 1]) + _dot_tn(v_bf[sl], kd[sl])
    return o + jnp.concatenate(inter, axis=0), st


def _unit_lower_inverse_m1(a, same_sub):
    d = jnp.where(same_sub, a, 0.0)
    l = a - d
    d2 = _mm(d, d)
    d4 = _mm(d2, d2)
    d8 = _mm(d4, d4)
    p = d2 - d - _mm(d, d2)
    p = p + d4 + _mm(p, d4)
    p = p + d8 + _mm(p, d8)
    n = l + _mm(p, l)
    n2 = _mm(n, n)
    x = n2 - n - _mm(n, n2)
    return x + p + _mm(x, p)


def _gdn_block(q, k, v, g_col, beta_col, s, row_in_chunk, lower_c, strict_c, same_sub):
    tb = q.shape[0]
    n_chunks = tb // CHUNK
    gc = _chunk_cumsum(jnp.broadcast_to(g_col, (tb, HEAD_DIM)), row_in_chunk)
    last = _bcast_rows(gc, [c * CHUNK + CHUNK - 1 for c in range(n_chunks)], CHUNK)
    gc_t = jnp.concatenate([gc] * (tb // HEAD_DIM), axis=1)
    gc_s = jnp.broadcast_to(gc.T[0:1], (tb, tb))
    decay = jnp.where(lower_c, jnp.exp(jnp.minimum(gc_t - gc_s, 0.0)), 0.0)
    kb = k * beta_col
    vb = v * beta_col
    kq = _dot_nt(_bf(jnp.concatenate([kb, q], axis=0)), _bf(k))
    a = jnp.where(strict_c, kq[:tb] * decay, 0.0)
    qk = kq[tb:] * decay
    t_m1 = _unit_lower_inverse_m1(a, same_sub)
    rhs = jnp.concatenate([vb, kb * jnp.exp(gc)], axis=1)
    uw = rhs + _mm(t_m1, rhs)
    u = uw[:, :HEAD_DIM]
    w = _bf(uw[:, HEAD_DIM:])
    qg = _bf(q * jnp.exp(gc))
    kd = _bf(k * jnp.exp(last - gc))
    inter, v_new = [], []
    for c in range(n_chunks):
        sl = slice(c * CHUNK, (c + 1) * CHUNK)
        s_bf = _bf(s)
        vn = u[sl] - _dot(w[sl], s_bf)
        inter.append(_dot(qg[sl], s_bf))
        v_new.append(vn)
        s = s * jnp.exp(last[c * CHUNK:c * CHUNK + 1]) + _dot_tn(kd[sl], _bf(vn))
    o = jnp.concatenate(inter, axis=0) + _mm(qk, jnp.concatenate(v_new, axis=0))
    return o, s


def _even_rec_kernel(aq_ref, af_ref, ai_ref, bq_ref, bk_ref, bv_ref, ga_ref, gb_ref, zs_ref,
                     lbc_ref, cwq_ref, cwk_ref, cwv_ref, gp_ref, hga_ref, hgb_ref,
                     ya_ref, yb_ref, st_ref, s_ref, carry_ref):
    h = pl.program_id(1)
    tb = aq_ref.shape[0]

    @pl.when(pl.program_id(2) == 0)
    def _():
        st_ref[...] = jnp.zeros_like(st_ref)
        s_ref[...] = jnp.zeros_like(s_ref)
        carry_ref[...] = jnp.zeros_like(carry_ref)

    row = lax.broadcasted_iota(jnp.int32, (tb, tb), 0)
    col = lax.broadcasted_iota(jnp.int32, (tb, tb), 1)
    chunk_sh = CHUNK.bit_length() - 1
    sub_sh = SUB.bit_length() - 1
    same_chunk = (row >> chunk_sh) == (col >> chunk_sh)
    same_sub = (row >> sub_sh) == (col >> sub_sh)
    lower_c = same_chunk & (row >= col)
    strict_c = same_chunk & (row > col)
    r_sub = (row >> sub_sh) & (CHUNK // SUB - 1)
    c_sub = (col >> sub_sh) & (CHUNK // SUB - 1)
    row_in_chunk = lax.broadcasted_iota(jnp.int32, (tb, HEAD_DIM), 0) & (CHUNK - 1)

    def head_out(o, hg, gate):
        y = o * lax.rsqrt(jnp.mean(o * o, axis=-1, keepdims=True) + EPS) * hg
        return _bf(y * _silu(gate))

    af = af_ref[...]
    log_lb = lbc_ref[0:1]
    log1m_lb = lbc_ref[1:2]
    onem_lb = lbc_ref[2:3]
    log_sig = jnp.minimum(af, 0.0) - jnp.log1p(jnp.exp(-jnp.abs(af)))
    bterm = log1m_lb + log_sig
    logf = jnp.maximum(log_lb, bterm) + jnp.log1p(jnp.exp(-jnp.abs(log_lb - bterm)))
    kh = onem_lb * _sigmoid(-af)
    o_a, st = _hgrn_block(aq_ref[...], kh, ai_ref[...], logf, st_ref[...], row_in_chunk,
                          same_chunk, r_sub, c_sub)
    st_ref[...] = st
    ya_ref[...] = head_out(o_a, hga_ref[0:1], ga_ref[...])

    xq = bq_ref[...]
    xk = bk_ref[...]
    xv = bv_ref[...]
    cq = _silu(_causal_conv(xq, carry_ref[0], cwq_ref[...]))
    ck = _silu(_causal_conv(xk, carry_ref[1], cwk_ref[...]))
    cv = _silu(_causal_conv(xv, carry_ref[2], cwv_ref[...]))
    carry_ref[0] = xq[tb - SUBLANES:]
    carry_ref[1] = xk[tb - SUBLANES:]
    carry_ref[2] = xv[tb - SUBLANES:]
    qn = cq * (lax.rsqrt(jnp.sum(cq * cq, axis=-1, keepdims=True) + EPS) * (HEAD_DIM ** -0.5))
    kn = ck * lax.rsqrt(jnp.sum(ck * ck, axis=-1, keepdims=True) + EPS)

    zs = zs_ref[...]
    lane = lax.broadcasted_iota(jnp.int32, zs.shape, 1)
    g_all = -jnp.exp(gp_ref[0:1]) * _softplus(zs + gp_ref[1:2])
    beta_all = _sigmoid(zs)
    n_heads = pl.num_programs(1)
    g_col = jnp.sum(jnp.where(lane == h, g_all, 0.0), axis=-1, keepdims=True)
    b_col = jnp.sum(jnp.where(lane == h + n_heads, beta_all, 0.0), axis=-1, keepdims=True)
    o_b, s = _gdn_block(qn, kn, cv, g_col, b_col, s_ref[...], row_in_chunk, lower_c, strict_c,
                        same_sub)
    s_ref[...] = s
    yb_ref[...] = head_out(o_b, hgb_ref[0:1], gb_ref[...])


def _even_rec(z, zs, lbc, cw, gp, hg, *, batch, seq, tb=256):
    m = z.shape[0]
    n_heads = lbc.shape[1] // HEAD_DIM
    tb = min(tb, seq)
    nt = seq // tb

    def zspec(k):
        return pl.BlockSpec((tb, HEAD_DIM), lambda b, h, t, k=k: (b * nt + t, k * n_heads + h))

    def pspec(k):
        return pl.BlockSpec((SUBLANES, HEAD_DIM), lambda b, h, t, k=k: (0, k * n_heads + h))

    in_specs = [zspec(k) for k in range(8)]
    in_specs.append(pl.BlockSpec((tb, LANES), lambda b, h, t: (b * nt + t, 0)))
    in_specs.append(pspec(0))
    in_specs += [pspec(0), pspec(1), pspec(2)]
    in_specs.append(pl.BlockSpec((SUBLANES, LANES), lambda b, h, t: (0, 0)))
    in_specs += [pspec(0), pspec(1)]
    out_spec = pl.BlockSpec((tb, HEAD_DIM), lambda b, h, t: (b * nt + t, h))
    return pl.pallas_call(
        _even_rec_kernel,
        grid=(batch, n_heads, nt),
        in_specs=in_specs,
        out_specs=[out_spec, out_spec],
        out_shape=[jax.ShapeDtypeStruct((m, n_heads * HEAD_DIM), _bf16)] * 2,
        scratch_shapes=[
            pltpu.VMEM((HEAD_DIM, HEAD_DIM), _f32),
            pltpu.VMEM((HEAD_DIM, HEAD_DIM), _f32),
            pltpu.VMEM((3, SUBLANES, HEAD_DIM), _f32),
        ],
        compiler_params=pltpu.CompilerParams(
            dimension_semantics=("parallel", "parallel", "arbitrary"),
            vmem_limit_bytes=VMEM_LIMIT),
        name="even_rec",
    )(*([z] * 8), zs, lbc, cw, cw, cw, gp, hg, hg)


def _odd_rec_kernel(xb_ref, gate_ref, cp_ref, wa_ref, wx_ref, y_ref, carry_ref, h_ref):
    t_idx = pl.program_id(2)
    tb = xb_ref.shape[0]

    @pl.when(t_idx == 0)
    def _():
        carry_ref[...] = jnp.zeros_like(carry_ref)
        h_ref[...] = jnp.zeros_like(h_ref)

    x = xb_ref[...]
    xc = _causal_conv(x, carry_ref[...], cp_ref[...]) + cp_ref[4:5]
    carry_ref[...] = x[tb - SUBLANES:]
    xc_bf = _bf(xc)
    r = _sigmoid(_dot(xc_bf, wa_ref[0]) + cp_ref[5:6])
    i = _sigmoid(_dot(xc_bf, wx_ref[0]) + cp_ref[6:7])
    log_a = -LRU_C * r * _softplus(-cp_ref[7:8])
    a = jnp.exp(log_a)
    mult = jnp.sqrt(-jnp.tanh(log_a) * (a * a + 1.0))
    row = lax.broadcasted_iota(jnp.int32, x.shape, 0)
    mult = jnp.where(row + t_idx * tb == 0, 1.0, mult)
    b = xc * i * mult

    sh = 1
    while sh < tb:
        a_sh = jnp.where(row >= sh, pltpu.roll(a, sh, 0), 1.0)
        b_sh = jnp.where(row >= sh, pltpu.roll(b, sh, 0), 0.0)
        b = a * b_sh + b
        a = a * a_sh
        sh *= 2
    hcur = b + a * h_ref[0:1]
    h_ref[...] = jnp.broadcast_to(hcur[tb - 1:tb], h_ref.shape)
    y_ref[...] = _bf(hcur * _silu(gate_ref[...]))


def _odd_rec(z, cp, wa_bf, wx_bf, *, batch, seq, tb=256):
    m = z.shape[0]
    n_blocks = wa_bf.shape[0]
    tb = min(tb, seq)
    nt = seq // tb
    in_specs = [
        pl.BlockSpec((tb, LANES), lambda b, g, t: (b * nt + t, g)),
        pl.BlockSpec((tb, LANES), lambda b, g, t: (b * nt + t, n_blocks + g)),
        pl.BlockSpec((SUBLANES, LANES), lambda b, g, t: (0, g)),
        pl.BlockSpec((1, LANES, LANES), lambda b, g, t: (g, 0, 0)),
        pl.BlockSpec((1, LANES, LANES), lambda b, g, t: (g, 0, 0)),
    ]
    return pl.pallas_call(
        _odd_rec_kernel,
        grid=(batch, n_blocks, nt),
        in_specs=in_specs,
        out_specs=pl.BlockSpec((tb, LANES), lambda b, g, t: (b * nt + t, g)),
        out_shape=jax.ShapeDtypeStruct((m, n_blocks * LANES), _bf16),
        scratch_shapes=[pltpu.VMEM((SUBLANES, LANES), _f32), pltpu.VMEM((SUBLANES, LANES), _f32)],
        compiler_params=pltpu.CompilerParams(
            dimension_semantics=("parallel", "parallel", "arbitrary"),
            vmem_limit_bytes=VMEM_LIMIT),
        name="odd_rec",
    )(z, z, cp, wa_bf, wx_bf)


def _pad_rows(a, rows=SUBLANES):
    return jnp.pad(a, ((0, rows - a.shape[0]), (0, 0)))


def kernel(x, p, even_norm, even_w_in, hgrn_lb_logits, gdn_conv_w, gdn_a_log, gdn_dt_bias,
           even_head_norm, even_w_out, odd_norm, odd_w_in, lru_conv_w, lru_conv_b, lru_w_a,
           lru_b_a, lru_w_x, lru_b_x, lru_lambda, odd_w_out, ple_norm, ple_gate_w, ple_proj_w,
           final_norm):
    batch, seq, d = x.shape
    depth = p.shape[0]
    m = batch * seq
    a_width = hgrn_lb_logits.shape[1]
    n_heads = a_width // HEAD_DIM
    qkv_end = 6 * a_width
    small_end = qkv_end + 2 * n_heads

    lb_cum = jnp.cumsum(jax.nn.softmax(hgrn_lb_logits.astype(_f32), axis=0), axis=0)
    lower_bounds = lb_cum - lb_cum[0]

    x2d = x.reshape(m, d)
    for i in range(depth):
        j = i // 2
        if i % 2 == 0:
            w_in = even_w_in[j]
            w_main = _bf(jnp.concatenate([w_in[:, :qkv_end], w_in[:, small_end:]], axis=1))
            w_small = _bf(jnp.pad(w_in[:, qkv_end:small_end], ((0, 0), (0, LANES - 2 * n_heads))))
            z, zs = _norm_inproj(x2d, even_norm[j], w_main, w_small)
            lb = lower_bounds[j]
            lbc = _pad_rows(jnp.stack([jnp.log(lb), jnp.log1p(-lb), 1.0 - lb]))
            cw = _pad_rows(gdn_conv_w[j].astype(_f32))
            gp = _pad_rows(jnp.pad(jnp.stack([gdn_a_log[j], gdn_dt_bias[j]]).astype(_f32),
                                   ((0, 0), (0, LANES - n_heads))))
            hg = _pad_rows(even_head_norm[j].astype(_f32)[None, :])
            ya, yb = _even_rec(z, zs, lbc, cw, gp, hg, batch=batch, seq=seq)
            w_out = _bf(even_w_out[j])
            ys, ws = [ya, yb], [w_out[:a_width], w_out[a_width:]]
        else:
            z = _norm_inproj(x2d, odd_norm[j], _bf(odd_w_in[j]))[0]
            cp = jnp.concatenate([
                lru_conv_w[j], lru_conv_b[j][None], lru_b_a[j][None], lru_b_x[j][None],
                lru_lambda[j][None]], axis=0).astype(_f32)
            y = _odd_rec(z, cp, _bf(lru_w_a[j]), _bf(lru_w_x[j]), batch=batch, seq=seq)
            ys, ws = [y], [_bf(odd_w_out[j])]
        x2d = _outproj_ple(ys, ws, x2d, p[i].reshape(m, -1), ple_norm[i], _bf(ple_gate_w[i]),
                           _bf(ple_proj_w[i]), final_norm if i == depth - 1 else None)
    return x2d.reshape(batch, seq, d)
```

```python
import functools

import jax
import jax.numpy as jnp
from jax import lax
from jax.experimental import pallas as pl
from jax.experimental.pallas import tpu as pltpu

EPS = 1e-6
HEAD_DIM = 128
LANES = 128
SUBLANES = 8
CHUNK = 64
SUB = 16
CONV_WIDTH = 4
LRU_C = 8.0
VMEM_LIMIT = 48 * 1024 * 1024

_f32 = jnp.float32
_bf16 = jnp.bfloat16


def _bf(x):
    return x.astype(_bf16)


def _dot(a, b):
    return jnp.dot(a, b, preferred_element_type=_f32)


def _dot_nt(a, b):
    return lax.dot_general(a, b, (((1,), (1,)), ((), ())), preferred_element_type=_f32)


def _dot_tn(a, b):
    return lax.dot_general(a, b, (((0,), (0,)), ((), ())), preferred_element_type=_f32)


def _mm(a, b):
    return _dot(_bf(a), _bf(b))


def _softplus(x):
    return jnp.maximum(x, 0.0) + jnp.log1p(jnp.exp(-jnp.abs(x)))


def _sigmoid(x):
    return jax.nn.sigmoid(x)


def _silu(x):
    return x * _sigmoid(x)


def _rmsnorm(x, g):
    ms = jnp.mean(x * x, axis=-1, keepdims=True)
    return x * lax.rsqrt(ms + EPS) * g


def _norm_inproj_kernel(*refs, has_small, n_small_heads):
    if has_small:
        x_ref, g_ref, w_ref, ws_ref, gp_ref, z_ref, zs_ref, xn_ref = refs
    else:
        x_ref, g_ref, w_ref, z_ref, xn_ref = refs

    @pl.when(pl.program_id(1) == 0)
    def _():
        xn_ref[...] = _bf(_rmsnorm(x_ref[...], g_ref[...]))
        if has_small:
            zs = _dot(xn_ref[...], ws_ref[...])
            g = -jnp.exp(gp_ref[0:1]) * _softplus(zs + gp_ref[1:2])
            lane = lax.broadcasted_iota(jnp.int32, zs.shape, 1)
            zs_ref[...] = jnp.where(lane < n_small_heads, g, _sigmoid(zs))

    z_ref[...] = _dot(xn_ref[...], w_ref[...])


def _norm_inproj(x2d, g, w_bf, ws_bf=None, gp=None, n_small_heads=0, *, tm=1024, tn=1024):
    m, d = x2d.shape
    n = w_bf.shape[1]
    tm = min(tm, m)
    has_small = ws_bf is not None
    in_specs = [
        pl.BlockSpec((tm, d), lambda i, j: (i, 0)),
        pl.BlockSpec((1, d), lambda i, j: (0, 0)),
        pl.BlockSpec((d, tn), lambda i, j: (0, j)),
    ]
    args = [x2d, g.reshape(1, d), w_bf]
    out_shape = [jax.ShapeDtypeStruct((m, n), _f32)]
    out_specs = [pl.BlockSpec((tm, tn), lambda i, j: (i, j))]
    if has_small:
        ns = ws_bf.shape[1]
        in_specs.append(pl.BlockSpec((d, ns), lambda i, j: (0, 0)))
        in_specs.append(pl.BlockSpec(gp.shape, lambda i, j: (0, 0)))
        args += [ws_bf, gp]
        out_shape.append(jax.ShapeDtypeStruct((m, ns), _f32))
        out_specs.append(pl.BlockSpec((tm, ns), lambda i, j: (i, 0)))
    return pl.pallas_call(
        functools.partial(_norm_inproj_kernel, has_small=has_small,
                          n_small_heads=n_small_heads),
        grid=(m // tm, n // tn),
        in_specs=in_specs,
        out_specs=out_specs,
        out_shape=out_shape,
        scratch_shapes=[pltpu.VMEM((tm, d), _bf16)],
        compiler_params=pltpu.CompilerParams(
            dimension_semantics=("parallel", "arbitrary"), vmem_limit_bytes=VMEM_LIMIT),
        name="norm_inproj",
    )(*args)


def _outproj_ple_kernel(*refs, n_y, final):
    y_refs = refs[:n_y]
    w_refs = refs[n_y:2 * n_y]
    x_ref, p_ref, png_ref, wg_ref, wp_ref = refs[2 * n_y:2 * n_y + 5]
    rest = refs[2 * n_y + 5:]
    if final:
        fng_ref, out_ref = rest
    else:
        (out_ref,) = rest
    acc = x_ref[...]
    for y_ref, w_ref in zip(y_refs, w_refs):
        acc = acc + _dot(y_ref[...], w_ref[...])
    xn = _rmsnorm(acc, png_ref[...])
    gate = _sigmoid(_dot(_bf(xn), wg_ref[...]))
    pp = _dot(_bf(p_ref[...]), wp_ref[...])
    x2 = acc + gate * pp
    if final:
        x2 = _rmsnorm(x2, fng_ref[...])
    out_ref[...] = x2


def _outproj_ple(ys, ws_bf, x2d, p2d, png, wg_bf, wp_bf, fng=None, *, tm=512):
    m, d = x2d.shape
    tm = min(tm, m)
    n_y = len(ys)
    final = fng is not None
    in_specs, args = [], []
    for y in ys:
        in_specs.append(pl.BlockSpec((tm, y.shape[1]), lambda i: (i, 0)))
        args.append(y)
    for w in ws_bf:
        in_specs.append(pl.BlockSpec(w.shape, lambda i: (0, 0)))
        args.append(w)
    in_specs += [
        pl.BlockSpec((tm, d), lambda i: (i, 0)),
        pl.BlockSpec((tm, p2d.shape[1]), lambda i: (i, 0)),
        pl.BlockSpec((1, d), lambda i: (0, 0)),
        pl.BlockSpec(wg_bf.shape, lambda i: (0, 0)),
        pl.BlockSpec(wp_bf.shape, lambda i: (0, 0)),
    ]
    args += [x2d, p2d, png.reshape(1, d), wg_bf, wp_bf]
    if final:
        in_specs.append(pl.BlockSpec((1, d), lambda i: (0, 0)))
        args.append(fng.reshape(1, d))
    return pl.pallas_call(
        functools.partial(_outproj_ple_kernel, n_y=n_y, final=final),
        grid=(m // tm,),
        in_specs=in_specs,
        out_specs=pl.BlockSpec((tm, d), lambda i: (i, 0)),
        out_shape=jax.ShapeDtypeStruct((m, d), _f32),
        compiler_params=pltpu.CompilerParams(
            dimension_semantics=("parallel",), vmem_limit_bytes=VMEM_LIMIT),
        name="outproj_ple",
    )(*args)


def _causal_conv(x, prev, w):
    t = x.shape[0]
    xp = jnp.concatenate([prev, x], axis=0)
    base = SUBLANES - (CONV_WIDTH - 1)
    y = w[CONV_WIDTH - 1:CONV_WIDTH] * x
    for j in range(CONV_WIDTH - 1):
        y = y + w[j:j + 1] * xp[base + j:base + j + t]
    return y


def _shift_rows(x, sh, fill):
    t = x.shape[0]
    pad = jnp.full((SUBLANES, x.shape[1]), fill, x.dtype)
    return jnp.concatenate([pad, x], axis=0)[SUBLANES - sh:SUBLANES - sh + t]


def _chunk_cumsum(x, row_in_chunk):
    sh = 1
    while sh < CHUNK:
        x = x + jnp.where(row_in_chunk >= sh, pltpu.roll(x, sh, 0), 0.0)
        sh *= 2
    return x


def _bcast_rows(x, starts, n):
    return jnp.concatenate(
        [jnp.broadcast_to(x[s:s + 1], (n, x.shape[1])) for s in starts], axis=0)


def _hgrn_block(q, k, v, logf, st, row_in_chunk, same_chunk, r_sub, c_sub):
    tb = q.shape[0]
    n_chunks = tb // CHUNK
    n_sub = CHUNK // SUB
    cum = _chunk_cumsum(logf, row_in_chunk)
    last = _bcast_rows(cum, [c * CHUNK + CHUNK - 1 for c in range(n_chunks)], CHUNK)

    anchor = _bcast_rows(cum, range(0, tb, SUB), SUB)
    qa = q * jnp.exp(cum - anchor)
    kas = []
    for i in range(1, n_sub):
        anc_i = _bcast_rows(cum, [c * CHUNK + i * SUB for c in range(n_chunks)], CHUNK)
        kas.append(k * jnp.exp(jnp.minimum(anc_i - cum, 0.0)))
    off = _dot_nt(_bf(qa), _bf(jnp.concatenate(kas, axis=0)))
    sel = off[:, (n_sub - 2) * tb:]
    for i in range(n_sub - 2, 0, -1):
        sel = jnp.where(r_sub == i, off[:, (i - 1) * tb:i * tb], sel)
    a_off = jnp.where(same_chunk, jnp.where(c_sub < r_sub, sel, 0.0), 0.0)
    o = _mm(a_off, v)

    qd = _bf(q * jnp.exp(cum))
    kd = _bf(k * jnp.exp(last - cum))
    v_bf = _bf(v)
    inter = []
    for c in range(n_chunks):
        sl = slice(c * CHUNK, (c + 1) * CHUNK)
        inter.append(_dot_nt(qd[sl], _bf(st)))
        st = st * jnp.exp(last[c * CHUNK:c * CHUNK + 1]) + _dot_tn(v_bf[sl], kd[sl])
    return o + jnp.concatenate(inter, axis=0), st, cum


def _zero_after(x):
    bits = lax.shift_right_logical(pltpu.bitcast(x, jnp.uint32), jnp.uint32(32))
    return pltpu.bitcast(bits, _f32)


def _hgrn_diag_unit(q_sub, k, v, cum, lo):
    row8 = lax.broadcasted_iota(jnp.int32, (SUBLANES, 1), 0)
    out = []
    for g in range(SUB // SUBLANES):
        top = lo + g * SUBLANES
        cs = cum[top:top + SUBLANES]
        qs = q_sub[g * SUBLANES:(g + 1) * SUBLANES]
        acc = jnp.zeros((SUBLANES, HEAD_DIM), _f32)
        for r in range(lo, top + SUBLANES):
            d = cs - cum[r:r + 1]
            if r >= top:
                d = jnp.minimum(d, 0.0)
            col = jnp.sum(qs * (k[r:r + 1] * jnp.exp(d)), axis=-1, keepdims=True)
            if r >= top:
                col = jnp.where(row8 >= r - top, col, 0.0)
            acc = acc + col * v[r:r + 1]
        out.append(acc)
    return jnp.concatenate(out, axis=0)


def _unit_lower_inverse_m1(a, same_sub):
    d = jnp.where(same_sub, a, 0.0)
    l = a - d
    d2 = _mm(d, d)
    yield d
    d4 = _mm(d2, d2)
    p = d2 - d - _mm(d, d2)
    yield d2
    d8 = _mm(d4, d4)
    p = p + d4 + _mm(p, d4)
    yield d4
    p = p + d8 + _mm(p, d8)
    yield d8
    n = l + _mm(p, l)
    yield p
    n2 = _mm(n, n)
    yield n
    x = n2 - n - _mm(n, n2)
    yield n2
    t = x + p + _mm(x, p)
    yield x
    return t


def _gdn_head(q, k, v, g_col, beta_col, row_in_chunk, lower_c, strict_c):
    tb = q.shape[0]
    n_chunks = tb // CHUNK
    gc = _chunk_cumsum(jnp.broadcast_to(g_col, (tb, HEAD_DIM)), row_in_chunk)
    last = _bcast_rows(gc, [c * CHUNK + CHUNK - 1 for c in range(n_chunks)], CHUNK)
    gc_t = jnp.concatenate([gc] * (tb // HEAD_DIM), axis=1)
    gc_s = jnp.broadcast_to(gc.T[0:1], (tb, tb))
    decay = jnp.where(lower_c, jnp.exp(jnp.minimum(gc_t - gc_s, 0.0)), 0.0)
    kb = k * beta_col
    vb = v * beta_col
    kq = _dot_nt(_bf(jnp.concatenate([kb, q], axis=0)), _bf(k))
    a = jnp.where(strict_c, kq[:tb] * decay, 0.0)
    qk = kq[tb:] * decay
    rhs = jnp.concatenate([vb, kb * jnp.exp(gc)], axis=1)
    qg = _bf(q * jnp.exp(gc))
    kd = _bf(k * jnp.exp(last - gc))
    return a, qk, rhs, qg, kd, last


def _gdn_chain(a, qk, rhs, qg, kd, last, s, same_sub, out):
    n_chunks = a.shape[0] // CHUNK
    t_m1 = yield from _unit_lower_inverse_m1(a, same_sub)
    uw = rhs + _mm(t_m1, rhs)
    yield t_m1
    u = uw[:, :HEAD_DIM]
    w = _bf(uw[:, HEAD_DIM:])
    inter, v_new = [], []
    for c in range(n_chunks):
        sl = slice(c * CHUNK, (c + 1) * CHUNK)
        s_bf = _bf(s)
        vn = u[sl] - _dot(w[sl], s_bf)
        yield s
        inter.append(_dot(qg[sl], s_bf))
        v_new.append(vn)
        s = s * jnp.exp(last[c * CHUNK:c * CHUNK + 1]) + _dot_tn(kd[sl], _bf(vn))
        yield vn
    out.append(jnp.concatenate(inter, axis=0) + _mm(qk, jnp.concatenate(v_new, axis=0)))
    out.append(s)


def _even_rec_kernel(aq_ref, af_ref, ai_ref, bq_ref, bk_ref, bv_ref, ga_ref, gb_ref, zs_ref,
                     lbc_ref, cwq_ref, cwk_ref, cwv_ref, hga_ref, hgb_ref,
                     ya_ref, yb_ref, st_ref, s_ref, carry_ref):
    tb = aq_ref.shape[0]
    hps = aq_ref.shape[1] // HEAD_DIM
    n_heads = pl.num_programs(1) * hps
    head0 = pl.program_id(1) * hps

    @pl.when(pl.program_id(2) == 0)
    def _():
        st_ref[...] = jnp.zeros_like(st_ref)
        s_ref[...] = jnp.zeros_like(s_ref)
        carry_ref[...] = jnp.zeros_like(carry_ref)

    row = lax.broadcasted_iota(jnp.int32, (tb, tb), 0)
    col = lax.broadcasted_iota(jnp.int32, (tb, tb), 1)
    chunk_sh = CHUNK.bit_length() - 1
    sub_sh = SUB.bit_length() - 1
    same_chunk = (row >> chunk_sh) == (col >> chunk_sh)
    same_sub = (row >> sub_sh) == (col >> sub_sh)
    lower_c = same_chunk & (row >= col)
    strict_c = same_chunk & (row > col)
    r_sub = (row >> sub_sh) & (CHUNK // SUB - 1)
    c_sub = (col >> sub_sh) & (CHUNK // SUB - 1)
    row_in_chunk = lax.broadcasted_iota(jnp.int32, (tb, HEAD_DIM), 0) & (CHUNK - 1)

    def head_out(o, hg, gate):
        y = o * lax.rsqrt(jnp.mean(o * o, axis=-1, keepdims=True) + EPS) * hg
        return _bf(y * _silu(gate))

    def lanes(hh):
        return slice(hh * HEAD_DIM, (hh + 1) * HEAD_DIM)

    zs = zs_ref[...]
    lane = lax.broadcasted_iota(jnp.int32, zs.shape, 1)
    chains, results = [], []
    for hh in range(hps):
        ln = lanes(hh)
        xq = bq_ref[:, ln]
        xk = bk_ref[:, ln]
        xv = bv_ref[:, ln]
        cq = _silu(_causal_conv(xq, carry_ref[hh, 0], cwq_ref[:, ln]))
        ck = _silu(_causal_conv(xk, carry_ref[hh, 1], cwk_ref[:, ln]))
        cv = _silu(_causal_conv(xv, carry_ref[hh, 2], cwv_ref[:, ln]))
        carry_ref[hh, 0] = xq[tb - SUBLANES:]
        carry_ref[hh, 1] = xk[tb - SUBLANES:]
        carry_ref[hh, 2] = xv[tb - SUBLANES:]
        qn = cq * (lax.rsqrt(jnp.sum(cq * cq, axis=-1, keepdims=True) + EPS)
                   * (HEAD_DIM ** -0.5))
        kn = ck * lax.rsqrt(jnp.sum(ck * ck, axis=-1, keepdims=True) + EPS)
        g_col = jnp.sum(jnp.where(lane == head0 + hh, zs, 0.0), axis=-1, keepdims=True)
        b_col = jnp.sum(jnp.where(lane == head0 + hh + n_heads, zs, 0.0), axis=-1,
                        keepdims=True)
        ops = _gdn_head(qn, kn, cv, g_col, b_col, row_in_chunk, lower_c, strict_c)
        res = []
        results.append(res)
        chains.append(_gdn_chain(*ops, s_ref[hh], same_sub, res))

    hgrn = []
    for hh in range(hps):
        ln = lanes(hh)
        af = af_ref[:, ln]
        log_lb = lbc_ref[0:1, ln]
        log1m_lb = lbc_ref[1:2, ln]
        onem_lb = lbc_ref[2:3, ln]
        log_sig = jnp.minimum(af, 0.0) - jnp.log1p(jnp.exp(-jnp.abs(af)))
        bterm = log1m_lb + log_sig
        logf = jnp.maximum(log_lb, bterm) + jnp.log1p(jnp.exp(-jnp.abs(log_lb - bterm)))
        kh = onem_lb * _sigmoid(-af)
        q_a = aq_ref[:, ln]
        v_a = ai_ref[:, ln]
        o_a, st, cum = _hgrn_block(q_a, kh, v_a, logf, st_ref[hh], row_in_chunk, same_chunk,
                                   r_sub, c_sub)
        st_ref[hh] = st
        hgrn.append((q_a, kh, v_a, cum, o_a, []))

    pending = [(hh, lo) for lo in range(0, tb, SUB) for hh in range(hps)]

    def fill(after=None):
        if pending:
            hh, lo = pending.pop(0)
            q_a, kh, v_a, cum, _, diag = hgrn[hh]
            qs = q_a[lo:lo + SUB]
            if after is not None:
                qs = qs + _zero_after(after[:SUB, :HEAD_DIM])
            diag.append(_hgrn_diag_unit(qs, kh, v_a, cum, lo))

    active = list(chains)
    while active:
        for chain in list(active):
            try:
                fill(next(chain))
            except StopIteration:
                active.remove(chain)
    while pending:
        fill()

    for hh in range(hps):
        ln = lanes(hh)
        o_b, s = results[hh]
        s_ref[hh] = s
        yb_ref[:, ln] = head_out(o_b, hgb_ref[0:1, ln], gb_ref[:, ln])
        o_a, diag = hgrn[hh][4], hgrn[hh][5]
        ya_ref[:, ln] = head_out(o_a + jnp.concatenate(diag, axis=0), hga_ref[0:1, ln],
                                 ga_ref[:, ln])


def _even_rec(z, zs, lbc, cw, hg, *, batch, seq, tb=256, hps=2):
    m = z.shape[0]
    n_heads = lbc.shape[1] // HEAD_DIM
    tb = min(tb, seq)
    nt = seq // tb
    width = hps * HEAD_DIM
    groups = n_heads // hps

    def zspec(k):
        return pl.BlockSpec((tb, width), lambda b, h, t, k=k: (b * nt + t, k * groups + h))

    def pspec(k):
        return pl.BlockSpec((SUBLANES, width), lambda b, h, t, k=k: (0, k * groups + h))

    in_specs = [zspec(k) for k in range(8)]
    in_specs.append(pl.BlockSpec((tb, LANES), lambda b, h, t: (b * nt + t, 0)))
    in_specs.append(pspec(0))
    in_specs += [pspec(0), pspec(1), pspec(2)]
    in_specs += [pspec(0), pspec(1)]
    out_spec = pl.BlockSpec((tb, width), lambda b, h, t: (b * nt + t, h))
    return pl.pallas_call(
        _even_rec_kernel,
        grid=(batch, groups, nt),
        in_specs=in_specs,
        out_specs=[out_spec, out_spec],
        out_shape=[jax.ShapeDtypeStruct((m, n_heads * HEAD_DIM), _bf16)] * 2,
        scratch_shapes=[
            pltpu.VMEM((hps, HEAD_DIM, HEAD_DIM), _f32),
            pltpu.VMEM((hps, HEAD_DIM, HEAD_DIM), _f32),
            pltpu.VMEM((hps, 3, SUBLANES, HEAD_DIM), _f32),
        ],
        compiler_params=pltpu.CompilerParams(
            dimension_semantics=("parallel", "parallel", "arbitrary"),
            vmem_limit_bytes=VMEM_LIMIT),
        name="even_rec",
    )(*([z] * 8), zs, lbc, cw, cw, cw, hg, hg)


def _odd_rec_kernel(xb_ref, gate_ref, cp_ref, wa_ref, wx_ref, y_ref, carry_ref, h_ref):
    t_idx = pl.program_id(2)
    tb = xb_ref.shape[0]

    @pl.when(t_idx == 0)
    def _():
        carry_ref[...] = jnp.zeros_like(carry_ref)
        h_ref[...] = jnp.zeros_like(h_ref)

    x = xb_ref[...]
    xc = _causal_conv(x, carry_ref[...], cp_ref[...]) + cp_ref[4:5]
    carry_ref[...] = x[tb - SUBLANES:]
    xc_bf = _bf(xc)
    r = _sigmoid(_dot(xc_bf, wa_ref[0]) + cp_ref[5:6])
    i = _sigmoid(_dot(xc_bf, wx_ref[0]) + cp_ref[6:7])
    log_a = -LRU_C * r * _softplus(-cp_ref[7:8])
    a = jnp.exp(log_a)
    mult = jnp.sqrt(-jnp.tanh(log_a) * (a * a + 1.0))
    row = lax.broadcasted_iota(jnp.int32, x.shape, 0)
    mult = jnp.where(row + t_idx * tb == 0, 1.0, mult)
    b = xc * i * mult

    sh = 1
    while sh < SUBLANES:
        a_sh = _shift_rows(a, sh, 1.0)
        b_sh = _shift_rows(b, sh, 0.0)
        b = a * b_sh + b
        a = a * a_sh
        sh *= 2
    h = h_ref[...]
    hs = []
    for lo in range(0, tb, SUBLANES):
        h = a[lo:lo + SUBLANES] * h + b[lo:lo + SUBLANES]
        hs.append(h)
    hcur = jnp.concatenate(hs, axis=0)
    h_ref[...] = jnp.broadcast_to(hcur[tb - 1:tb], h_ref.shape)
    y_ref[...] = _bf(hcur * _silu(gate_ref[...]))


def _odd_rec(z, cp, wa_bf, wx_bf, *, batch, seq, tb=512):
    m = z.shape[0]
    n_blocks = wa_bf.shape[0]
    tb = min(tb, seq)
    nt = seq // tb
    in_specs = [
        pl.BlockSpec((tb, LANES), lambda b, g, t: (b * nt + t, g)),
        pl.BlockSpec((tb, LANES), lambda b, g, t: (b * nt + t, n_blocks + g)),
        pl.BlockSpec((SUBLANES, LANES), lambda b, g, t: (0, g)),
        pl.BlockSpec((1, LANES, LANES), lambda b, g, t: (g, 0, 0)),
        pl.BlockSpec((1, LANES, LANES), lambda b, g, t: (g, 0, 0)),
    ]
    return pl.pallas_call(
        _odd_rec_kernel,
        grid=(batch, n_blocks, nt),
        in_specs=in_specs,
        out_specs=pl.BlockSpec((tb, LANES), lambda b, g, t: (b * nt + t, g)),
        out_shape=jax.ShapeDtypeStruct((m, n_blocks * LANES), _bf16),
        scratch_shapes=[pltpu.VMEM((SUBLANES, LANES), _f32), pltpu.VMEM((SUBLANES, LANES), _f32)],
        compiler_params=pltpu.CompilerParams(
            dimension_semantics=("parallel", "parallel", "arbitrary"),
            vmem_limit_bytes=VMEM_LIMIT),
        name="odd_rec",
    )(z, z, cp, wa_bf, wx_bf)


def _pad_rows(a, rows=SUBLANES):
    return jnp.pad(a, ((0, rows - a.shape[0]), (0, 0)))


def kernel(x, p, even_norm, even_w_in, hgrn_lb_logits, gdn_conv_w, gdn_a_log, gdn_dt_bias,
           even_head_norm, even_w_out, odd_norm, odd_w_in, lru_conv_w, lru_conv_b, lru_w_a,
           lru_b_a, lru_w_x, lru_b_x, lru_lambda, odd_w_out, ple_norm, ple_gate_w, ple_proj_w,
           final_norm):
    batch, seq, d = x.shape
    depth = p.shape[0]
    m = batch * seq
    a_width = hgrn_lb_logits.shape[1]
    n_heads = a_width // HEAD_DIM
    qkv_end = 6 * a_width
    small_end = qkv_end + 2 * n_heads

    lb_cum = jnp.cumsum(jax.nn.softmax(hgrn_lb_logits.astype(_f32), axis=0), axis=0)
    lower_bounds = lb_cum - lb_cum[0]

    x2d = x.reshape(m, d)
    for i in range(depth):
        j = i // 2
        if i % 2 == 0:
            w_in = even_w_in[j]
            w_main = _bf(jnp.concatenate([w_in[:, :qkv_end], w_in[:, small_end:]], axis=1))
            w_small = _bf(jnp.pad(w_in[:, qkv_end:small_end], ((0, 0), (0, LANES - 2 * n_heads))))
            gp = _pad_rows(jnp.pad(jnp.stack([gdn_a_log[j], gdn_dt_bias[j]]).astype(_f32),
                                   ((0, 0), (0, LANES - n_heads))))
            z, zs = _norm_inproj(x2d, even_norm[j], w_main, w_small, gp, n_heads)
            lb = lower_bounds[j]
            lbc = _pad_rows(jnp.stack([jnp.log(lb), jnp.log1p(-lb), 1.0 - lb]))
            cw = _pad_rows(gdn_conv_w[j].astype(_f32))
            hg = _pad_rows(even_head_norm[j].astype(_f32)[None, :])
            ya, yb = _even_rec(z, zs, lbc, cw, hg, batch=batch, seq=seq)
            w_out = _bf(even_w_out[j])
            ys, ws = [ya, yb], [w_out[:a_width], w_out[a_width:]]
        else:
            z = _norm_inproj(x2d, odd_norm[j], _bf(odd_w_in[j]))[0]
            cp = jnp.concatenate([
                lru_conv_w[j], lru_conv_b[j][None], lru_b_a[j][None], lru_b_x[j][None],
                lru_lambda[j][None]], axis=0).astype(_f32)
            y = _odd_rec(z, cp, _bf(lru_w_a[j]), _bf(lru_w_x[j]), batch=batch, seq=seq)
            ys, ws = [y], [_bf(odd_w_out[j])]
        x2d = _outproj_ple(ys, ws, x2d, p[i].reshape(m, -1), ple_norm[i], _bf(ple_gate_w[i]),
                           _bf(ple_proj_w[i]), final_norm if i == depth - 1 else None)
    return x2d.reshape(batch, seq, d)
```

```python
import functools

import jax
import jax.numpy as jnp
from jax import lax
from jax.experimental import pallas as pl
from jax.experimental.pallas import tpu as pltpu

EPS = 1e-6
HEAD_DIM = 128
LANES = 128
SUBLANES = 8
CHUNK = 64
SUB = 16
CONV_WIDTH = 4
LRU_C = 8.0
LOG2E = 1.4426950408889634
VMEM_LIMIT = 48 * 1024 * 1024

_f32 = jnp.float32
_bf16 = jnp.bfloat16


def _bf(x):
    return x.astype(_bf16)


def _dot(a, b):
    return jnp.dot(a, b, preferred_element_type=_f32)


def _dot_nt(a, b):
    return lax.dot_general(a, b, (((1,), (1,)), ((), ())), preferred_element_type=_f32)


def _dot_tn(a, b):
    return lax.dot_general(a, b, (((0,), (0,)), ((), ())), preferred_element_type=_f32)


def _mm(a, b):
    return _dot(_bf(a), _bf(b))


def _softplus(x):
    return jnp.maximum(x, 0.0) + jnp.log1p(jnp.exp(-jnp.abs(x)))


def _sigmoid(x):
    return jax.nn.sigmoid(x)


def _silu(x):
    return x * _sigmoid(x)


def _rmsnorm(x, g):
    ms = jnp.mean(x * x, axis=-1, keepdims=True)
    return x * lax.rsqrt(ms + EPS) * g


def _norm_inproj_kernel(*refs, has_small, n_small_heads):
    if has_small:
        x_ref, g_ref, w_ref, ws_ref, gp_ref, z_ref, zs_ref, xn_ref = refs
    else:
        x_ref, g_ref, w_ref, z_ref, xn_ref = refs

    @pl.when(pl.program_id(1) == 0)
    def _():
        xn_ref[...] = _bf(_rmsnorm(x_ref[...], g_ref[...]))
        if has_small:
            zs = _dot(xn_ref[...], ws_ref[...])
            g = -jnp.exp(gp_ref[0:1]) * _softplus(zs + gp_ref[1:2])
            lane = lax.broadcasted_iota(jnp.int32, zs.shape, 1)
            zs_ref[...] = jnp.where(lane < n_small_heads, g, _sigmoid(zs))

    z_ref[...] = _dot(xn_ref[...], w_ref[...])


def _norm_inproj(x2d, g, w_bf, ws_bf=None, gp=None, n_small_heads=0, *, tm=1024, tn=1024):
    m, d = x2d.shape
    n = w_bf.shape[1]
    tm = min(tm, m)
    has_small = ws_bf is not None
    in_specs = [
        pl.BlockSpec((tm, d), lambda i, j: (i, 0)),
        pl.BlockSpec((1, d), lambda i, j: (0, 0)),
        pl.BlockSpec((d, tn), lambda i, j: (0, j)),
    ]
    args = [x2d, g.reshape(1, d), w_bf]
    out_shape = [jax.ShapeDtypeStruct((m, n), _f32)]
    out_specs = [pl.BlockSpec((tm, tn), lambda i, j: (i, j))]
    if has_small:
        ns = ws_bf.shape[1]
        in_specs.append(pl.BlockSpec((d, ns), lambda i, j: (0, 0)))
        in_specs.append(pl.BlockSpec(gp.shape, lambda i, j: (0, 0)))
        args += [ws_bf, gp]
        out_shape.append(jax.ShapeDtypeStruct((m, ns), _f32))
        out_specs.append(pl.BlockSpec((tm, ns), lambda i, j: (i, 0)))
    return pl.pallas_call(
        functools.partial(_norm_inproj_kernel, has_small=has_small,
                          n_small_heads=n_small_heads),
        grid=(m // tm, n // tn),
        in_specs=in_specs,
        out_specs=out_specs,
        out_shape=out_shape,
        scratch_shapes=[pltpu.VMEM((tm, d), _bf16)],
        compiler_params=pltpu.CompilerParams(
            dimension_semantics=("parallel", "arbitrary"), vmem_limit_bytes=VMEM_LIMIT),
        name="norm_inproj",
    )(*args)


def _outproj_ple_kernel(*refs, n_y, final):
    y_refs = refs[:n_y]
    w_refs = refs[n_y:2 * n_y]
    x_ref, p_ref, png_ref, wg_ref, wp_ref = refs[2 * n_y:2 * n_y + 5]
    rest = refs[2 * n_y + 5:]
    if final:
        fng_ref, out_ref = rest
    else:
        (out_ref,) = rest
    acc = x_ref[...]
    for y_ref, w_ref in zip(y_refs, w_refs):
        acc = acc + _dot(y_ref[...], w_ref[...])
    xn = _rmsnorm(acc, png_ref[...])
    gate = _sigmoid(_dot(_bf(xn), wg_ref[...]))
    pp = _dot(_bf(p_ref[...]), wp_ref[...])
    x2 = acc + gate * pp
    if final:
        x2 = _rmsnorm(x2, fng_ref[...])
    out_ref[...] = x2


def _outproj_ple(ys, ws_bf, x2d, p2d, png, wg_bf, wp_bf, fng=None, *, tm=512):
    m, d = x2d.shape
    tm = min(tm, m)
    n_y = len(ys)
    final = fng is not None
    in_specs, args = [], []
    for y in ys:
        in_specs.append(pl.BlockSpec((tm, y.shape[1]), lambda i: (i, 0)))
        args.append(y)
    for w in ws_bf:
        in_specs.append(pl.BlockSpec(w.shape, lambda i: (0, 0)))
        args.append(w)
    in_specs += [
        pl.BlockSpec((tm, d), lambda i: (i, 0)),
        pl.BlockSpec((tm, p2d.shape[1]), lambda i: (i, 0)),
        pl.BlockSpec((1, d), lambda i: (0, 0)),
        pl.BlockSpec(wg_bf.shape, lambda i: (0, 0)),
        pl.BlockSpec(wp_bf.shape, lambda i: (0, 0)),
    ]
    args += [x2d, p2d, png.reshape(1, d), wg_bf, wp_bf]
    if final:
        in_specs.append(pl.BlockSpec((1, d), lambda i: (0, 0)))
        args.append(fng.reshape(1, d))
    return pl.pallas_call(
        functools.partial(_outproj_ple_kernel, n_y=n_y, final=final),
        grid=(m // tm,),
        in_specs=in_specs,
        out_specs=pl.BlockSpec((tm, d), lambda i: (i, 0)),
        out_shape=jax.ShapeDtypeStruct((m, d), _f32),
        compiler_params=pltpu.CompilerParams(
            dimension_semantics=("parallel",), vmem_limit_bytes=VMEM_LIMIT),
        name="outproj_ple",
    )(*args)


def _shift_rows_from(x, prev, sh):
    t = x.shape[0]
    return jnp.concatenate([prev, x], axis=0)[SUBLANES - sh:SUBLANES - sh + t]


def _causal_conv(x, prev, w):
    y = w[CONV_WIDTH - 1:CONV_WIDTH] * x
    for j in range(CONV_WIDTH - 1):
        y = y + w[j:j + 1] * _shift_rows_from(x, prev, CONV_WIDTH - 1 - j)
    return y


def _shift_rows(x, sh, fill):
    return _shift_rows_from(x, jnp.full((SUBLANES, x.shape[1]), fill, x.dtype), sh)


def _chunk_cumsum(x, row_in_chunk):
    sh = 1
    while sh < CHUNK:
        x = x + jnp.where(row_in_chunk >= sh, pltpu.roll(x, sh, 0), 0.0)
        sh *= 2
    return x


def _bcast_rows(x, starts, n):
    return jnp.concatenate(
        [jnp.broadcast_to(x[s:s + 1], (n, x.shape[1])) for s in starts], axis=0)


def _hgrn_block(q, k, v, logf, st, row_in_chunk, same_chunk, r_sub, c_sub):
    tb = q.shape[0]
    n_chunks = tb // CHUNK
    n_sub = CHUNK // SUB
    cum = _chunk_cumsum(logf, row_in_chunk)
    last = _bcast_rows(cum, [c * CHUNK + CHUNK - 1 for c in range(n_chunks)], CHUNK)

    anchor = _bcast_rows(cum, range(0, tb, SUB), SUB)
    qa = q * jnp.exp2(cum - anchor)
    kas = []
    for i in range(1, n_sub):
        anc_i = _bcast_rows(cum, [c * CHUNK + i * SUB for c in range(n_chunks)], CHUNK)
        kas.append(k * jnp.exp2(jnp.minimum(anc_i - cum, 0.0)))
    off = _dot_nt(_bf(qa), _bf(jnp.concatenate(kas, axis=0)))
    sel = off[:, (n_sub - 2) * tb:]
    for i in range(n_sub - 2, 0, -1):
        sel = jnp.where(r_sub == i, off[:, (i - 1) * tb:i * tb], sel)
    a_off = jnp.where(same_chunk, jnp.where(c_sub < r_sub, sel, 0.0), 0.0)
    o = _mm(a_off, v)

    qd = _bf(q * jnp.exp2(cum))
    kd = _bf(k * jnp.exp2(last - cum))
    v_bf = _bf(v)
    inter = []
    for c in range(n_chunks):
        sl = slice(c * CHUNK, (c + 1) * CHUNK)
        inter.append(_dot_nt(qd[sl], _bf(st)))
        st = st * jnp.exp2(last[c * CHUNK:c * CHUNK + 1]) + _dot_tn(v_bf[sl], kd[sl])
    return o + jnp.concatenate(inter, axis=0), st, cum


def _zero_after(x):
    bits = lax.shift_right_logical(pltpu.bitcast(x, jnp.uint32), jnp.uint32(32))
    return pltpu.bitcast(bits, _f32)


def _hgrn_diag_unit(q_sub, rows_ref, cum, lo):
    row8 = lax.broadcasted_iota(jnp.int32, (SUBLANES, 1), 0)
    out = []
    for g in range(SUB // SUBLANES):
        top = lo + g * SUBLANES
        cs = cum[top:top + SUBLANES]
        qs = q_sub[g * SUBLANES:(g + 1) * SUBLANES]
        acc = jnp.zeros((SUBLANES, HEAD_DIM), _f32)
        for r in range(lo, top + SUBLANES):
            d = cs - rows_ref[0, r:r + 1, :]
            if r >= top:
                d = jnp.minimum(d, 0.0)
            col = jnp.sum(qs * (rows_ref[1, r:r + 1, :] * jnp.exp2(d)), axis=-1, keepdims=True)
            if r >= top:
                col = jnp.where(row8 >= r - top, col, 0.0)
            acc = acc + col * rows_ref[2, r:r + 1, :]
        out.append(acc)
    return jnp.concatenate(out, axis=0)


def _unit_lower_inverse_m1(a, same_sub):
    d = jnp.where(same_sub, a, 0.0)
    l = a - d
    d2 = _mm(d, d)
    yield d
    d4 = _mm(d2, d2)
    p = d2 - d - _mm(d, d2)
    yield d2
    d8 = _mm(d4, d4)
    p = p + d4 + _mm(p, d4)
    yield d4
    p = p + d8 + _mm(p, d8)
    yield d8
    n = l + _mm(p, l)
    yield p
    n2 = _mm(n, n)
    yield n
    x = n2 - n - _mm(n, n2)
    yield n2
    t = x + p + _mm(x, p)
    yield x
    return t


def _gdn_head(q, k, v, g_col, beta_col, row_in_chunk, lower_c, strict_c):
    tb = q.shape[0]
    n_chunks = tb // CHUNK
    gc = _chunk_cumsum(jnp.broadcast_to(g_col, (tb, HEAD_DIM)), row_in_chunk)
    last = _bcast_rows(gc, [c * CHUNK + CHUNK - 1 for c in range(n_chunks)], CHUNK)
    gc_t = jnp.concatenate([gc] * (tb // HEAD_DIM), axis=1)
    gc_s = jnp.broadcast_to(gc.T[0:1], (tb, tb))
    decay = jnp.where(lower_c, jnp.exp2(jnp.minimum(gc_t - gc_s, 0.0)), 0.0)
    kb = k * beta_col
    vb = v * beta_col
    kq = _dot_nt(_bf(jnp.concatenate([kb, q], axis=0)), _bf(k))
    a = jnp.where(strict_c, kq[:tb] * decay, 0.0)
    qk = kq[tb:] * decay
    rhs = jnp.concatenate([vb, kb * jnp.exp2(gc)], axis=1)
    qg = _bf(q * jnp.exp2(gc))
    kd = _bf(k * jnp.exp2(last - gc))
    return a, qk, rhs, qg, kd, last


def _gdn_chain(a, qk, rhs, qg, kd, last, s, same_sub, out):
    n_chunks = a.shape[0] // CHUNK
    t_m1 = yield from _unit_lower_inverse_m1(a, same_sub)
    uw = rhs + _mm(t_m1, rhs)
    yield t_m1
    u = uw[:, :HEAD_DIM]
    w = _bf(uw[:, HEAD_DIM:])
    inter, v_new = [], []
    for c in range(n_chunks):
        sl = slice(c * CHUNK, (c + 1) * CHUNK)
        s_bf = _bf(s)
        vn = u[sl] - _dot(w[sl], s_bf)
        yield s
        inter.append(_dot(qg[sl], s_bf))
        v_new.append(vn)
        s = s * jnp.exp2(last[c * CHUNK:c * CHUNK + 1]) + _dot_tn(kd[sl], _bf(vn))
        yield vn
    out.append(jnp.concatenate(inter, axis=0) + _mm(qk, jnp.concatenate(v_new, axis=0)))
    out.append(s)


def _even_rec_kernel(aq_ref, af_ref, ai_ref, bq_ref, bk_ref, bv_ref, ga_ref, gb_ref, zs_ref,
                     lbc_ref, cwq_ref, cwk_ref, cwv_ref, hga_ref, hgb_ref,
                     ya_ref, yb_ref, st_ref, s_ref, carry_ref, rows_ref):
    tb = aq_ref.shape[0]
    hps = aq_ref.shape[1] // HEAD_DIM
    n_heads = pl.num_programs(1) * hps
    head0 = pl.program_id(1) * hps

    @pl.when(pl.program_id(2) == 0)
    def _():
        st_ref[...] = jnp.zeros_like(st_ref)
        s_ref[...] = jnp.zeros_like(s_ref)
        carry_ref[...] = jnp.zeros_like(carry_ref)

    row = lax.broadcasted_iota(jnp.int32, (tb, tb), 0)
    col = lax.broadcasted_iota(jnp.int32, (tb, tb), 1)
    chunk_sh = CHUNK.bit_length() - 1
    sub_sh = SUB.bit_length() - 1
    same_chunk = (row >> chunk_sh) == (col >> chunk_sh)
    same_sub = (row >> sub_sh) == (col >> sub_sh)
    lower_c = same_chunk & (row >= col)
    strict_c = same_chunk & (row > col)
    r_sub = (row >> sub_sh) & (CHUNK // SUB - 1)
    c_sub = (col >> sub_sh) & (CHUNK // SUB - 1)
    row_in_chunk = lax.broadcasted_iota(jnp.int32, (tb, HEAD_DIM), 0) & (CHUNK - 1)

    def head_out(o, hg, gate):
        y = o * lax.rsqrt(jnp.mean(o * o, axis=-1, keepdims=True) + EPS) * hg
        return _bf(y * _silu(gate))

    def lanes(hh):
        return slice(hh * HEAD_DIM, (hh + 1) * HEAD_DIM)

    zs = zs_ref[...]
    lane = lax.broadcasted_iota(jnp.int32, zs.shape, 1)
    chains, results = [], []
    for hh in range(hps):
        ln = lanes(hh)
        xq = bq_ref[:, ln]
        xk = bk_ref[:, ln]
        xv = bv_ref[:, ln]
        cq = _silu(_causal_conv(xq, carry_ref[hh, 0], cwq_ref[:, ln]))
        ck = _silu(_causal_conv(xk, carry_ref[hh, 1], cwk_ref[:, ln]))
        cv = _silu(_causal_conv(xv, carry_ref[hh, 2], cwv_ref[:, ln]))
        carry_ref[hh, 0] = xq[tb - SUBLANES:]
        carry_ref[hh, 1] = xk[tb - SUBLANES:]
        carry_ref[hh, 2] = xv[tb - SUBLANES:]
        qn = cq * (lax.rsqrt(jnp.sum(cq * cq, axis=-1, keepdims=True) + EPS)
                   * (HEAD_DIM ** -0.5))
        kn = ck * lax.rsqrt(jnp.sum(ck * ck, axis=-1, keepdims=True) + EPS)
        g_col = jnp.sum(jnp.where(lane == head0 + hh, zs, 0.0), axis=-1, keepdims=True) * LOG2E
        b_col = jnp.sum(jnp.where(lane == head0 + hh + n_heads, zs, 0.0), axis=-1,
                        keepdims=True)
        ops = _gdn_head(qn, kn, cv, g_col, b_col, row_in_chunk, lower_c, strict_c)
        res = []
        results.append(res)
        chains.append(_gdn_chain(*ops, s_ref[hh], same_sub, res))

    hgrn = []
    for hh in range(hps):
        ln = lanes(hh)
        af = af_ref[:, ln]
        log_lb = lbc_ref[0:1, ln]
        log1m_lb = lbc_ref[1:2, ln]
        onem_lb = lbc_ref[2:3, ln]
        log_sig = jnp.minimum(af, 0.0) - jnp.log(1.0 + jnp.exp(-jnp.abs(af)))
        bterm = log1m_lb + log_sig
        logf = (jnp.maximum(log_lb, bterm)
                + jnp.log(1.0 + jnp.exp(-jnp.abs(log_lb - bterm)))) * LOG2E
        kh = onem_lb * _sigmoid(-af)
        q_a = aq_ref[:, ln]
        v_a = ai_ref[:, ln]
        o_a, st, cum = _hgrn_block(q_a, kh, v_a, logf, st_ref[hh], row_in_chunk, same_chunk,
                                   r_sub, c_sub)
        st_ref[hh] = st
        rows_ref[hh, 0] = cum
        rows_ref[hh, 1] = kh
        rows_ref[hh, 2] = v_a
        hgrn.append((q_a, cum, o_a, []))

    pending = [(hh, lo) for lo in range(0, tb, SUB) for hh in range(hps)]

    def fill(after=None):
        if pending:
            hh, lo = pending.pop(0)
            q_a, cum, _, diag = hgrn[hh]
            qs = q_a[lo:lo + SUB]
            if after is not None:
                qs = qs + _zero_after(after[:SUB, :HEAD_DIM])
            diag.append(_hgrn_diag_unit(qs, rows_ref.at[hh], cum, lo))

    active = list(chains)
    while active:
        for chain in list(active):
            try:
                fill(next(chain))
            except StopIteration:
                active.remove(chain)
    while pending:
        fill()

    for hh in range(hps):
        ln = lanes(hh)
        o_b, s = results[hh]
        s_ref[hh] = s
        yb_ref[:, ln] = head_out(o_b, hgb_ref[0:1, ln], gb_ref[:, ln])
        o_a, diag = hgrn[hh][2], hgrn[hh][3]
        ya_ref[:, ln] = head_out(o_a + jnp.concatenate(diag, axis=0), hga_ref[0:1, ln],
                                 ga_ref[:, ln])


def _even_rec(z, zs, lbc, cw, hg, *, batch, seq, tb=128, hps=8):
    m = z.shape[0]
    n_heads = lbc.shape[1] // HEAD_DIM
    tb = min(tb, seq)
    nt = seq // tb
    width = hps * HEAD_DIM
    groups = n_heads // hps

    def zspec(k):
        return pl.BlockSpec((tb, width), lambda b, h, t, k=k: (b * nt + t, k * groups + h))

    def pspec(k):
        return pl.BlockSpec((SUBLANES, width), lambda b, h, t, k=k: (0, k * groups + h))

    in_specs = [zspec(k) for k in range(8)]
    in_specs.append(pl.BlockSpec((tb, LANES), lambda b, h, t: (b * nt + t, 0)))
    in_specs.append(pspec(0))
    in_specs += [pspec(0), pspec(1), pspec(2)]
    in_specs += [pspec(0), pspec(1)]
    out_spec = pl.BlockSpec((tb, width), lambda b, h, t: (b * nt + t, h))
    return pl.pallas_call(
        _even_rec_kernel,
        grid=(batch, groups, nt),
        in_specs=in_specs,
        out_specs=[out_spec, out_spec],
        out_shape=[jax.ShapeDtypeStruct((m, n_heads * HEAD_DIM), _bf16)] * 2,
        scratch_shapes=[
            pltpu.VMEM((hps, HEAD_DIM, HEAD_DIM), _f32),
            pltpu.VMEM((hps, HEAD_DIM, HEAD_DIM), _f32),
            pltpu.VMEM((hps, 3, SUBLANES, HEAD_DIM), _f32),
            pltpu.VMEM((hps, 3, tb, HEAD_DIM), _f32),
        ],
        compiler_params=pltpu.CompilerParams(
            dimension_semantics=("parallel", "parallel", "arbitrary"),
            vmem_limit_bytes=VMEM_LIMIT),
        name="even_rec",
    )(*([z] * 8), zs, lbc, cw, cw, cw, hg, hg)


def _odd_rec_kernel(xb_ref, gate_ref, cp_ref, wa_ref, wx_ref, y_ref, carry_ref, h_ref):
    t_idx = pl.program_id(2)
    tb = xb_ref.shape[0]
    bps = xb_ref.shape[1] // LANES

    @pl.when(t_idx == 0)
    def _():
        carry_ref[...] = jnp.zeros_like(carry_ref)
        h_ref[...] = jnp.zeros_like(h_ref)

    first_row = (lax.broadcasted_iota(jnp.int32, (tb, LANES), 0) + t_idx * tb) == 0
    for blk in range(bps):
        ln = slice(blk * LANES, (blk + 1) * LANES)
        x = xb_ref[:, ln]
        cp = cp_ref[:, ln]
        xc = _causal_conv(x, carry_ref[blk], cp) + cp[4:5]
        carry_ref[blk] = x[tb - SUBLANES:]
        xc_bf = _bf(xc)
        r = _sigmoid(_dot(xc_bf, wa_ref[blk]) + cp[5:6])
        i = _sigmoid(_dot(xc_bf, wx_ref[blk]) + cp[6:7])
        log_a = -LRU_C * r * _softplus(-cp[7:8])
        a = jnp.exp(log_a)
        m2 = -jnp.tanh(log_a) * (a * a + 1.0)
        mult = jnp.where(m2 > 0.0, m2 * lax.rsqrt(m2), 0.0)
        mult = jnp.where(first_row, 1.0, mult)
        b = xc * i * mult

        sh = 1
        while sh < SUBLANES:
            a_sh = _shift_rows(a, sh, 1.0)
            b_sh = _shift_rows(b, sh, 0.0)
            b = a * b_sh + b
            a = a * a_sh
            sh *= 2
        h = h_ref[blk]
        hs = []
        for lo in range(0, tb, SUBLANES):
            h = a[lo:lo + SUBLANES] * h + b[lo:lo + SUBLANES]
            hs.append(h)
        hcur = jnp.concatenate(hs, axis=0)
        h_ref[blk] = jnp.broadcast_to(hcur[tb - 1:tb], (SUBLANES, LANES))
        y_ref[:, ln] = _bf(hcur * _silu(gate_ref[:, ln]))


def _odd_rec(z, cp, wa_bf, wx_bf, *, batch, seq, tb=512, bps=2):
    m = z.shape[0]
    n_blocks = wa_bf.shape[0]
    tb = min(tb, seq)
    nt = seq // tb
    groups = n_blocks // bps
    width = bps * LANES
    in_specs = [
        pl.BlockSpec((tb, width), lambda b, g, t: (b * nt + t, g)),
        pl.BlockSpec((tb, width), lambda b, g, t: (b * nt + t, groups + g)),
        pl.BlockSpec((SUBLANES, width), lambda b, g, t: (0, g)),
        pl.BlockSpec((bps, LANES, LANES), lambda b, g, t: (g, 0, 0)),
        pl.BlockSpec((bps, LANES, LANES), lambda b, g, t: (g, 0, 0)),
    ]
    return pl.pallas_call(
        _odd_rec_kernel,
        grid=(batch, groups, nt),
        in_specs=in_specs,
        out_specs=pl.BlockSpec((tb, width), lambda b, g, t: (b * nt + t, g)),
        out_shape=jax.ShapeDtypeStruct((m, n_blocks * LANES), _bf16),
        scratch_shapes=[pltpu.VMEM((bps, SUBLANES, LANES), _f32),
                        pltpu.VMEM((bps, SUBLANES, LANES), _f32)],
        compiler_params=pltpu.CompilerParams(
            dimension_semantics=("parallel", "parallel", "arbitrary"),
            vmem_limit_bytes=VMEM_LIMIT),
        name="odd_rec",
    )(z, z, cp, wa_bf, wx_bf)


def _pad_rows(a, rows=SUBLANES):
    return jnp.pad(a, ((0, rows - a.shape[0]), (0, 0)))


def kernel(x, p, even_norm, even_w_in, hgrn_lb_logits, gdn_conv_w, gdn_a_log, gdn_dt_bias,
           even_head_norm, even_w_out, odd_norm, odd_w_in, lru_conv_w, lru_conv_b, lru_w_a,
           lru_b_a, lru_w_x, lru_b_x, lru_lambda, odd_w_out, ple_norm, ple_gate_w, ple_proj_w,
           final_norm):
    batch, seq, d = x.shape
    depth = p.shape[0]
    m = batch * seq
    a_width = hgrn_lb_logits.shape[1]
    n_heads = a_width // HEAD_DIM
    qkv_end = 6 * a_width
    small_end = qkv_end + 2 * n_heads

    lb_cum = jnp.cumsum(jax.nn.softmax(hgrn_lb_logits.astype(_f32), axis=0), axis=0)
    lower_bounds = lb_cum - lb_cum[0]

    x2d = x.reshape(m, d)
    for i in range(depth):
        j = i // 2
        if i % 2 == 0:
            w_in = even_w_in[j]
            w_main = _bf(jnp.concatenate([w_in[:, :qkv_end], w_in[:, small_end:]], axis=1))
            w_small = _bf(jnp.pad(w_in[:, qkv_end:small_end], ((0, 0), (0, LANES - 2 * n_heads))))
            gp = _pad_rows(jnp.pad(jnp.stack([gdn_a_log[j], gdn_dt_bias[j]]).astype(_f32),
                                   ((0, 0), (0, LANES - n_heads))))
            z, zs = _norm_inproj(x2d, even_norm[j], w_main, w_small, gp, n_heads)
            lb = lower_bounds[j]
            lbc = _pad_rows(jnp.stack([jnp.log(lb), jnp.log1p(-lb), 1.0 - lb]))
            cw = _pad_rows(gdn_conv_w[j].astype(_f32))
            hg = _pad_rows(even_head_norm[j].astype(_f32)[None, :])
            ya, yb = _even_rec(z, zs, lbc, cw, hg, batch=batch, seq=seq)
            w_out = _bf(even_w_out[j])
            ys, ws = [ya, yb], [w_out[:a_width], w_out[a_width:]]
        else:
            z = _norm_inproj(x2d, odd_norm[j], _bf(odd_w_in[j]))[0]
            cp = jnp.concatenate([
                lru_conv_w[j], lru_conv_b[j][None], lru_b_a[j][None], lru_b_x[j][None],
                lru_lambda[j][None]], axis=0).astype(_f32)
            y = _odd_rec(z, cp, _bf(lru_w_a[j]), _bf(lru_w_x[j]), batch=batch, seq=seq)
            ys, ws = [y], [_bf(odd_w_out[j])]
        x2d = _outproj_ple(ys, ws, x2d, p[i].reshape(m, -1), ple_norm[i], _bf(ple_gate_w[i]),
                           _bf(ple_proj_w[i]), final_norm if i == depth - 1 else None)
    return x2d.reshape(batch, seq, d)
```

```python
import functools

import jax
import jax.numpy as jnp
from jax import lax
from jax.experimental import pallas as pl
from jax.experimental.pallas import tpu as pltpu

EPS = 1e-6
HEAD_DIM = 128
LANES = 128
SUBLANES = 8
CHUNK = 64
SUB = 16
CONV_WIDTH = 4
LRU_C = 8.0
LOG2E = 1.4426950408889634
VMEM_LIMIT = 48 * 1024 * 1024

_f32 = jnp.float32
_bf16 = jnp.bfloat16


def _bf(x):
    return x.astype(_bf16)


def _dot(a, b):
    return jnp.dot(a, b, preferred_element_type=_f32)


def _dot_nt(a, b):
    return lax.dot_general(a, b, (((1,), (1,)), ((), ())), preferred_element_type=_f32)


def _dot_tn(a, b):
    return lax.dot_general(a, b, (((0,), (0,)), ((), ())), preferred_element_type=_f32)


def _mm(a, b):
    return _dot(_bf(a), _bf(b))


def _softplus(x):
    return jnp.maximum(x, 0.0) + jnp.log1p(jnp.exp(-jnp.abs(x)))


def _sigmoid(x):
    return jax.nn.sigmoid(x)


def _silu(x):
    return x * _sigmoid(x)


def _rmsnorm(x, g):
    ms = jnp.mean(x * x, axis=-1, keepdims=True)
    return x * lax.rsqrt(ms + EPS) * g


def _norm_inproj_kernel(*refs, has_small, n_small_heads):
    if has_small:
        x_ref, g_ref, w_ref, ws_ref, gp_ref, z_ref, zs_ref, xn_ref = refs
    else:
        x_ref, g_ref, w_ref, z_ref, xn_ref = refs

    @pl.when(pl.program_id(1) == 0)
    def _():
        xn_ref[...] = _bf(_rmsnorm(x_ref[...], g_ref[...]))
        if has_small:
            zs = _dot(xn_ref[...], ws_ref[...])
            g = -jnp.exp(gp_ref[0:1]) * _softplus(zs + gp_ref[1:2])
            lane = lax.broadcasted_iota(jnp.int32, zs.shape, 1)
            zs_ref[...] = jnp.where(lane < n_small_heads, g, _sigmoid(zs))

    z_ref[...] = _bf(_dot(xn_ref[...], w_ref[...]))


def _norm_inproj(x2d, g, w_bf, ws_bf=None, gp=None, n_small_heads=0, *, tm=2048, tn=1024):
    m, d = x2d.shape
    n = w_bf.shape[1]
    tm = min(tm, m)
    has_small = ws_bf is not None
    in_specs = [
        pl.BlockSpec((tm, d), lambda i, j: (i, 0)),
        pl.BlockSpec((1, d), lambda i, j: (0, 0)),
        pl.BlockSpec((d, tn), lambda i, j: (0, j)),
    ]
    args = [x2d, g.reshape(1, d), w_bf]
    out_shape = [jax.ShapeDtypeStruct((m, n), _bf16)]
    out_specs = [pl.BlockSpec((tm, tn), lambda i, j: (i, j))]
    if has_small:
        ns = ws_bf.shape[1]
        in_specs.append(pl.BlockSpec((d, ns), lambda i, j: (0, 0)))
        in_specs.append(pl.BlockSpec(gp.shape, lambda i, j: (0, 0)))
        args += [ws_bf, gp]
        out_shape.append(jax.ShapeDtypeStruct((m, ns), _f32))
        out_specs.append(pl.BlockSpec((tm, ns), lambda i, j: (i, 0)))
    return pl.pallas_call(
        functools.partial(_norm_inproj_kernel, has_small=has_small,
                          n_small_heads=n_small_heads),
        grid=(m // tm, n // tn),
        in_specs=in_specs,
        out_specs=out_specs,
        out_shape=out_shape,
        scratch_shapes=[pltpu.VMEM((tm, d), _bf16)],
        compiler_params=pltpu.CompilerParams(
            dimension_semantics=("parallel", "arbitrary"), vmem_limit_bytes=VMEM_LIMIT),
        name="norm_inproj",
    )(*args)


def _outproj_ple_kernel(*refs, n_y, final):
    y_refs = refs[:n_y]
    w_refs = refs[n_y:2 * n_y]
    x_ref, p_ref, png_ref, wg_ref, wp_ref = refs[2 * n_y:2 * n_y + 5]
    rest = refs[2 * n_y + 5:]
    if final:
        fng_ref, out_ref = rest
    else:
        (out_ref,) = rest
    acc = x_ref[...]
    for y_ref, w_ref in zip(y_refs, w_refs):
        acc = acc + _dot(y_ref[...], w_ref[...])
    xn = _rmsnorm(acc, png_ref[...])
    gate = _sigmoid(_dot(_bf(xn), wg_ref[...]))
    pp = _dot(_bf(p_ref[...]), wp_ref[...])
    x2 = acc + gate * pp
    if final:
        x2 = _rmsnorm(x2, fng_ref[...])
    out_ref[...] = x2


def _outproj_ple(ys, ws_bf, x2d, p2d, png, wg_bf, wp_bf, fng=None, *, tm=512):
    m, d = x2d.shape
    tm = min(tm, m)
    n_y = len(ys)
    final = fng is not None
    in_specs, args = [], []
    for y in ys:
        in_specs.append(pl.BlockSpec((tm, y.shape[1]), lambda i: (i, 0)))
        args.append(y)
    for w in ws_bf:
        in_specs.append(pl.BlockSpec(w.shape, lambda i: (0, 0)))
        args.append(w)
    in_specs += [
        pl.BlockSpec((tm, d), lambda i: (i, 0)),
        pl.BlockSpec((tm, p2d.shape[1]), lambda i: (i, 0)),
        pl.BlockSpec((1, d), lambda i: (0, 0)),
        pl.BlockSpec(wg_bf.shape, lambda i: (0, 0)),
        pl.BlockSpec(wp_bf.shape, lambda i: (0, 0)),
    ]
    args += [x2d, p2d, png.reshape(1, d), wg_bf, wp_bf]
    if final:
        in_specs.append(pl.BlockSpec((1, d), lambda i: (0, 0)))
        args.append(fng.reshape(1, d))
    return pl.pallas_call(
        functools.partial(_outproj_ple_kernel, n_y=n_y, final=final),
        grid=(m // tm,),
        in_specs=in_specs,
        out_specs=pl.BlockSpec((tm, d), lambda i: (i, 0)),
        out_shape=jax.ShapeDtypeStruct((m, d), _f32),
        compiler_params=pltpu.CompilerParams(
            dimension_semantics=("parallel",), vmem_limit_bytes=VMEM_LIMIT),
        name="outproj_ple",
    )(*args)


def _shift_rows_from(x, prev, sh):
    t = x.shape[0]
    return jnp.concatenate([prev, x], axis=0)[SUBLANES - sh:SUBLANES - sh + t]


def _causal_conv(x, prev, w):
    y = w[CONV_WIDTH - 1:CONV_WIDTH] * x
    for j in range(CONV_WIDTH - 1):
        y = y + w[j:j + 1] * _shift_rows_from(x, prev, CONV_WIDTH - 1 - j)
    return y


def _shift_rows(x, sh, fill):
    return _shift_rows_from(x, jnp.full((SUBLANES, x.shape[1]), fill, x.dtype), sh)


def _chunk_cumsum(x, row_in_chunk):
    sh = 1
    while sh < CHUNK:
        x = x + jnp.where(row_in_chunk >= sh, pltpu.roll(x, sh, 0), 0.0)
        sh *= 2
    return x


def _bcast_rows(x, starts, n):
    return jnp.concatenate(
        [jnp.broadcast_to(x[s:s + 1], (n, x.shape[1])) for s in starts], axis=0)


def _hgrn_block(q, k, v, logf, st, row_in_chunk, same_chunk, r_sub, c_sub):
    tb = q.shape[0]
    n_chunks = tb // CHUNK
    n_sub = CHUNK // SUB
    cum = _chunk_cumsum(logf, row_in_chunk)
    last = _bcast_rows(cum, [c * CHUNK + CHUNK - 1 for c in range(n_chunks)], CHUNK)

    anchor = _bcast_rows(cum, range(0, tb, SUB), SUB)
    qa = q * jnp.exp2(cum - anchor)
    kas = []
    for i in range(1, n_sub):
        anc_i = _bcast_rows(cum, [c * CHUNK + i * SUB for c in range(n_chunks)], CHUNK)
        kas.append(k * jnp.exp2(jnp.minimum(anc_i - cum, 0.0)))
    off = _dot_nt(_bf(qa), _bf(jnp.concatenate(kas, axis=0)))
    sel = off[:, (n_sub - 2) * tb:]
    for i in range(n_sub - 2, 0, -1):
        sel = jnp.where(r_sub == i, off[:, (i - 1) * tb:i * tb], sel)
    a_off = jnp.where(same_chunk, jnp.where(c_sub < r_sub, sel, 0.0), 0.0)
    o = _mm(a_off, v)

    qd = _bf(q * jnp.exp2(cum))
    kd = _bf(k * jnp.exp2(last - cum))
    v_bf = _bf(v)
    inter = []
    for c in range(n_chunks):
        sl = slice(c * CHUNK, (c + 1) * CHUNK)
        inter.append(_dot_nt(qd[sl], _bf(st)))
        st = st * jnp.exp2(last[c * CHUNK:c * CHUNK + 1]) + _dot_tn(v_bf[sl], kd[sl])
    return o + jnp.concatenate(inter, axis=0), st, cum


def _zero_after(x):
    bits = lax.shift_right_logical(pltpu.bitcast(x, jnp.uint32), jnp.uint32(32))
    return pltpu.bitcast(bits, _f32)


def _hgrn_diag_unit(q_sub, rows_ref, cum, lo):
    row8 = lax.broadcasted_iota(jnp.int32, (SUBLANES, 1), 0)
    out = []
    for g in range(SUB // SUBLANES):
        top = lo + g * SUBLANES
        cs = cum[top:top + SUBLANES]
        qs = q_sub[g * SUBLANES:(g + 1) * SUBLANES]
        acc = jnp.zeros((SUBLANES, HEAD_DIM), _f32)
        for r in range(lo, top + SUBLANES):
            d = cs - rows_ref[0, r:r + 1, :]
            if r >= top:
                d = jnp.minimum(d, 0.0)
            col = jnp.sum(qs * (rows_ref[1, r:r + 1, :] * jnp.exp2(d)), axis=-1, keepdims=True)
            if r >= top:
                col = jnp.where(row8 >= r - top, col, 0.0)
            acc = acc + col * rows_ref[2, r:r + 1, :]
        out.append(acc)
    return jnp.concatenate(out, axis=0)


def _unit_lower_inverse_m1(a, same_sub):
    d = jnp.where(same_sub, a, 0.0)
    l = a - d
    d2 = _mm(d, d)
    yield d
    d4 = _mm(d2, d2)
    p = d2 - d - _mm(d, d2)
    yield d2
    d8 = _mm(d4, d4)
    p = p + d4 + _mm(p, d4)
    yield d4
    p = p + d8 + _mm(p, d8)
    yield d8
    n = l + _mm(p, l)
    yield p
    n2 = _mm(n, n)
    yield n
    x = n2 - n - _mm(n, n2)
    yield n2
    t = x + p + _mm(x, p)
    yield x
    return t


def _gdn_head(q, k, v, g_col, beta_col, row_in_chunk, lower_c, strict_c):
    tb = q.shape[0]
    n_chunks = tb // CHUNK
    gc = _chunk_cumsum(jnp.broadcast_to(g_col, (tb, HEAD_DIM)), row_in_chunk)
    last = _bcast_rows(gc, [c * CHUNK + CHUNK - 1 for c in range(n_chunks)], CHUNK)
    gc_t = jnp.concatenate([gc] * (tb // HEAD_DIM), axis=1)
    gc_s = jnp.broadcast_to(gc.T[0:1], (tb, tb))
    decay = jnp.where(lower_c, jnp.exp2(jnp.minimum(gc_t - gc_s, 0.0)), 0.0)
    kb = k * beta_col
    vb = v * beta_col
    kq = _dot_nt(_bf(jnp.concatenate([kb, q], axis=0)), _bf(k))
    a = jnp.where(strict_c, kq[:tb] * decay, 0.0)
    qk = kq[tb:] * decay
    rhs = jnp.concatenate([vb, kb * jnp.exp2(gc)], axis=1)
    qg = _bf(q * jnp.exp2(gc))
    kd = _bf(k * jnp.exp2(last - gc))
    return a, qk, rhs, qg, kd, last


def _gdn_chain(a, qk, rhs, qg, kd, last, s, same_sub, out):
    n_chunks = a.shape[0] // CHUNK
    t_m1 = yield from _unit_lower_inverse_m1(a, same_sub)
    uw = rhs + _mm(t_m1, rhs)
    yield t_m1
    u = uw[:, :HEAD_DIM]
    w = _bf(uw[:, HEAD_DIM:])
    inter, v_new = [], []
    for c in range(n_chunks):
        sl = slice(c * CHUNK, (c + 1) * CHUNK)
        s_bf = _bf(s)
        vn = u[sl] - _dot(w[sl], s_bf)
        yield s
        inter.append(_dot(qg[sl], s_bf))
        v_new.append(vn)
        s = s * jnp.exp2(last[c * CHUNK:c * CHUNK + 1]) + _dot_tn(kd[sl], _bf(vn))
        yield vn
    out.append(jnp.concatenate(inter, axis=0) + _mm(qk, jnp.concatenate(v_new, axis=0)))
    out.append(s)


def _even_rec_kernel(aq_ref, af_ref, ai_ref, bq_ref, bk_ref, bv_ref, ga_ref, gb_ref, zs_ref,
                     lbc_ref, cwq_ref, cwk_ref, cwv_ref, hga_ref, hgb_ref,
                     ya_ref, yb_ref, st_ref, s_ref, carry_ref, rows_ref):
    tb = aq_ref.shape[0]
    hps = aq_ref.shape[1] // HEAD_DIM
    n_heads = pl.num_programs(1) * hps
    head0 = pl.program_id(1) * hps

    @pl.when(pl.program_id(2) == 0)
    def _():
        st_ref[...] = jnp.zeros_like(st_ref)
        s_ref[...] = jnp.zeros_like(s_ref)
        carry_ref[...] = jnp.zeros_like(carry_ref)

    row = lax.broadcasted_iota(jnp.int32, (tb, tb), 0)
    col = lax.broadcasted_iota(jnp.int32, (tb, tb), 1)
    chunk_sh = CHUNK.bit_length() - 1
    sub_sh = SUB.bit_length() - 1
    same_chunk = (row >> chunk_sh) == (col >> chunk_sh)
    same_sub = (row >> sub_sh) == (col >> sub_sh)
    lower_c = same_chunk & (row >= col)
    strict_c = same_chunk & (row > col)
    r_sub = (row >> sub_sh) & (CHUNK // SUB - 1)
    c_sub = (col >> sub_sh) & (CHUNK // SUB - 1)
    row_in_chunk = lax.broadcasted_iota(jnp.int32, (tb, HEAD_DIM), 0) & (CHUNK - 1)

    def head_out(o, hg, gate):
        y = o * lax.rsqrt(jnp.mean(o * o, axis=-1, keepdims=True) + EPS) * hg
        return _bf(y * _silu(gate))

    def lanes(hh):
        return slice(hh * HEAD_DIM, (hh + 1) * HEAD_DIM)

    zs = zs_ref[...]
    lane = lax.broadcasted_iota(jnp.int32, zs.shape, 1)
    chains, results = [], []
    for hh in range(hps):
        ln = lanes(hh)
        xq = bq_ref[:, ln].astype(_f32)
        xk = bk_ref[:, ln].astype(_f32)
        xv = bv_ref[:, ln].astype(_f32)
        cq = _silu(_causal_conv(xq, carry_ref[hh, 0], cwq_ref[:, ln]))
        ck = _silu(_causal_conv(xk, carry_ref[hh, 1], cwk_ref[:, ln]))
        cv = _silu(_causal_conv(xv, carry_ref[hh, 2], cwv_ref[:, ln]))
        carry_ref[hh, 0] = xq[tb - SUBLANES:]
        carry_ref[hh, 1] = xk[tb - SUBLANES:]
        carry_ref[hh, 2] = xv[tb - SUBLANES:]
        qn = cq * (lax.rsqrt(jnp.sum(cq * cq, axis=-1, keepdims=True) + EPS)
                   * (HEAD_DIM ** -0.5))
        kn = ck * lax.rsqrt(jnp.sum(ck * ck, axis=-1, keepdims=True) + EPS)
        g_col = jnp.sum(jnp.where(lane == head0 + hh, zs, 0.0), axis=-1, keepdims=True) * LOG2E
        b_col = jnp.sum(jnp.where(lane == head0 + hh + n_heads, zs, 0.0), axis=-1,
                        keepdims=True)
        ops = _gdn_head(qn, kn, cv, g_col, b_col, row_in_chunk, lower_c, strict_c)
        res = []
        results.append(res)
        chains.append(_gdn_chain(*ops, s_ref[hh], same_sub, res))

    hgrn = []
    for hh in range(hps):
        ln = lanes(hh)
        af = af_ref[:, ln].astype(_f32)
        log_lb = lbc_ref[0:1, ln]
        log1m_lb = lbc_ref[1:2, ln]
        onem_lb = lbc_ref[2:3, ln]
        log_sig = jnp.minimum(af, 0.0) - jnp.log(1.0 + jnp.exp(-jnp.abs(af)))
        bterm = log1m_lb + log_sig
        logf = (jnp.maximum(log_lb, bterm)
                + jnp.log(1.0 + jnp.exp(-jnp.abs(log_lb - bterm)))) * LOG2E
        kh = onem_lb * _sigmoid(-af)
        q_a = aq_ref[:, ln].astype(_f32)
        v_a = ai_ref[:, ln].astype(_f32)
        o_a, st, cum = _hgrn_block(q_a, kh, v_a, logf, st_ref[hh], row_in_chunk, same_chunk,
                                   r_sub, c_sub)
        st_ref[hh] = st
        rows_ref[hh, 0] = cum
        rows_ref[hh, 1] = kh
        rows_ref[hh, 2] = v_a
        hgrn.append((q_a, cum, o_a, []))

    pending = [(hh, lo) for lo in range(0, tb, SUB) for hh in range(hps)]

    def fill(after=None):
        if pending:
            hh, lo = pending.pop(0)
            q_a, cum, _, diag = hgrn[hh]
            qs = q_a[lo:lo + SUB]
            if after is not None:
                qs = qs + _zero_after(after[:SUB, :HEAD_DIM])
            diag.append(_hgrn_diag_unit(qs, rows_ref.at[hh], cum, lo))

    active = list(chains)
    while active:
        for chain in list(active):
            try:
                fill(next(chain))
            except StopIteration:
                active.remove(chain)
    while pending:
        fill()

    for hh in range(hps):
        ln = lanes(hh)
        o_b, s = results[hh]
        s_ref[hh] = s
        yb_ref[:, ln] = head_out(o_b, hgb_ref[0:1, ln], gb_ref[:, ln].astype(_f32))
        o_a, diag = hgrn[hh][2], hgrn[hh][3]
        ya_ref[:, ln] = head_out(o_a + jnp.concatenate(diag, axis=0), hga_ref[0:1, ln],
                                 ga_ref[:, ln].astype(_f32))


def _even_rec(z, zs, lbc, cw, hg, *, batch, seq, tb=128, hps=8):
    m = z.shape[0]
    n_heads = lbc.shape[1] // HEAD_DIM
    tb = min(tb, seq)
    nt = seq // tb
    width = hps * HEAD_DIM
    groups = n_heads // hps

    def zspec(k):
        return pl.BlockSpec((tb, width), lambda b, h, t, k=k: (b * nt + t, k * groups + h))

    def pspec(k):
        return pl.BlockSpec((SUBLANES, width), lambda b, h, t, k=k: (0, k * groups + h))

    in_specs = [zspec(k) for k in range(8)]
    in_specs.append(pl.BlockSpec((tb, LANES), lambda b, h, t: (b * nt + t, 0)))
    in_specs.append(pspec(0))
    in_specs += [pspec(0), pspec(1), pspec(2)]
    in_specs += [pspec(0), pspec(1)]
    out_spec = pl.BlockSpec((tb, width), lambda b, h, t: (b * nt + t, h))
    return pl.pallas_call(
        _even_rec_kernel,
        grid=(batch, groups, nt),
        in_specs=in_specs,
        out_specs=[out_spec, out_spec],
        out_shape=[jax.ShapeDtypeStruct((m, n_heads * HEAD_DIM), _bf16)] * 2,
        scratch_shapes=[
            pltpu.VMEM((hps, HEAD_DIM, HEAD_DIM), _f32),
            pltpu.VMEM((hps, HEAD_DIM, HEAD_DIM), _f32),
            pltpu.VMEM((hps, 3, SUBLANES, HEAD_DIM), _f32),
            pltpu.VMEM((hps, 3, tb, HEAD_DIM), _f32),
        ],
        compiler_params=pltpu.CompilerParams(
            dimension_semantics=("parallel", "parallel", "arbitrary"),
            vmem_limit_bytes=VMEM_LIMIT),
        name="even_rec",
    )(*([z] * 8), zs, lbc, cw, cw, cw, hg, hg)


def _odd_rec_kernel(xb_ref, gate_ref, cp_ref, wa_ref, wx_ref, y_ref, carry_ref, h_ref):
    t_idx = pl.program_id(2)
    tb = xb_ref.shape[0]
    bps = xb_ref.shape[1] // LANES

    @pl.when(t_idx == 0)
    def _():
        carry_ref[...] = jnp.zeros_like(carry_ref)
        h_ref[...] = jnp.zeros_like(h_ref)

    first_row = (lax.broadcasted_iota(jnp.int32, (tb, LANES), 0) + t_idx * tb) == 0
    for blk in range(bps):
        ln = slice(blk * LANES, (blk + 1) * LANES)
        x = xb_ref[:, ln].astype(_f32)
        cp = cp_ref[:, ln]
        xc = _causal_conv(x, carry_ref[blk], cp) + cp[4:5]
        carry_ref[blk] = x[tb - SUBLANES:]
        xc_bf = _bf(xc)
        r = _sigmoid(_dot(xc_bf, wa_ref[blk]) + cp[5:6])
        i = _sigmoid(_dot(xc_bf, wx_ref[blk]) + cp[6:7])
        log_a = -LRU_C * r * _softplus(-cp[7:8])
        a = jnp.exp(log_a)
        m2 = -jnp.tanh(log_a) * (a * a + 1.0)
        mult = jnp.where(m2 > 0.0, m2 * lax.rsqrt(m2), 0.0)
        mult = jnp.where(first_row, 1.0, mult)
        b = xc * i * mult

        sh = 1
        while sh < SUBLANES:
            a_sh = _shift_rows(a, sh, 1.0)
            b_sh = _shift_rows(b, sh, 0.0)
            b = a * b_sh + b
            a = a * a_sh
            sh *= 2
        h = h_ref[blk]
        hs = []
        for lo in range(0, tb, SUBLANES):
            h = a[lo:lo + SUBLANES] * h + b[lo:lo + SUBLANES]
            hs.append(h)
        hcur = jnp.concatenate(hs, axis=0)
        h_ref[blk] = jnp.broadcast_to(hcur[tb - 1:tb], (SUBLANES, LANES))
        y_ref[:, ln] = _bf(hcur * _silu(gate_ref[:, ln].astype(_f32)))


def _odd_rec(z, cp, wa_bf, wx_bf, *, batch, seq, tb=512, bps=2):
    m = z.shape[0]
    n_blocks = wa_bf.shape[0]
    tb = min(tb, seq)
    nt = seq // tb
    groups = n_blocks // bps
    width = bps * LANES
    in_specs = [
        pl.BlockSpec((tb, width), lambda b, g, t: (b * nt + t, g)),
        pl.BlockSpec((tb, width), lambda b, g, t: (b * nt + t, groups + g)),
        pl.BlockSpec((SUBLANES, width), lambda b, g, t: (0, g)),
        pl.BlockSpec((bps, LANES, LANES), lambda b, g, t: (g, 0, 0)),
        pl.BlockSpec((bps, LANES, LANES), lambda b, g, t: (g, 0, 0)),
    ]
    return pl.pallas_call(
        _odd_rec_kernel,
        grid=(batch, groups, nt),
        in_specs=in_specs,
        out_specs=pl.BlockSpec((tb, width), lambda b, g, t: (b * nt + t, g)),
        out_shape=jax.ShapeDtypeStruct((m, n_blocks * LANES), _bf16),
        scratch_shapes=[pltpu.VMEM((bps, SUBLANES, LANES), _f32),
                        pltpu.VMEM((bps, SUBLANES, LANES), _f32)],
        compiler_params=pltpu.CompilerParams(
            dimension_semantics=("parallel", "parallel", "arbitrary"),
            vmem_limit_bytes=VMEM_LIMIT),
        name="odd_rec",
    )(z, z, cp, wa_bf, wx_bf)


def _pad_rows(a, rows=SUBLANES):
    return jnp.pad(a, ((0, rows - a.shape[0]), (0, 0)))


def kernel(x, p, even_norm, even_w_in, hgrn_lb_logits, gdn_conv_w, gdn_a_log, gdn_dt_bias,
           even_head_norm, even_w_out, odd_norm, odd_w_in, lru_conv_w, lru_conv_b, lru_w_a,
           lru_b_a, lru_w_x, lru_b_x, lru_lambda, odd_w_out, ple_norm, ple_gate_w, ple_proj_w,
           final_norm):
    batch, seq, d = x.shape
    depth = p.shape[0]
    m = batch * seq
    a_width = hgrn_lb_logits.shape[1]
    n_heads = a_width // HEAD_DIM
    qkv_end = 6 * a_width
    small_end = qkv_end + 2 * n_heads

    lb_cum = jnp.cumsum(jax.nn.softmax(hgrn_lb_logits.astype(_f32), axis=0), axis=0)
    lower_bounds = lb_cum - lb_cum[0]

    x2d = x.reshape(m, d)
    for i in range(depth):
        j = i // 2
        if i % 2 == 0:
            w_in = even_w_in[j]
            w_main = _bf(jnp.concatenate([w_in[:, :qkv_end], w_in[:, small_end:]], axis=1))
            w_small = _bf(jnp.pad(w_in[:, qkv_end:small_end], ((0, 0), (0, LANES - 2 * n_heads))))
            gp = _pad_rows(jnp.pad(jnp.stack([gdn_a_log[j], gdn_dt_bias[j]]).astype(_f32),
                                   ((0, 0), (0, LANES - n_heads))))
            z, zs = _norm_inproj(x2d, even_norm[j], w_main, w_small, gp, n_heads)
            lb = lower_bounds[j]
            lbc = _pad_rows(jnp.stack([jnp.log(lb), jnp.log1p(-lb), 1.0 - lb]))
            cw = _pad_rows(gdn_conv_w[j].astype(_f32))
            hg = _pad_rows(even_head_norm[j].astype(_f32)[None, :])
            ya, yb = _even_rec(z, zs, lbc, cw, hg, batch=batch, seq=seq)
            w_out = _bf(even_w_out[j])
            ys, ws = [ya, yb], [w_out[:a_width], w_out[a_width:]]
        else:
            z = _norm_inproj(x2d, odd_norm[j], _bf(odd_w_in[j]))[0]
            cp = jnp.concatenate([
                lru_conv_w[j], lru_conv_b[j][None], lru_b_a[j][None], lru_b_x[j][None],
                lru_lambda[j][None]], axis=0).astype(_f32)
            y = _odd_rec(z, cp, _bf(lru_w_a[j]), _bf(lru_w_x[j]), batch=batch, seq=seq)
            ys, ws = [y], [_bf(odd_w_out[j])]
        x2d = _outproj_ple(ys, ws, x2d, p[i].reshape(m, -1), ple_norm[i], _bf(ple_gate_w[i]),
                           _bf(ple_proj_w[i]), final_norm if i == depth - 1 else None)
    return x2d.reshape(batch, seq, d)
```
